```python
import math
import jax
import jax.numpy as jnp
from jax import lax
import numpy as np

D_MODEL = 4096
BATCH = 2
SEQ = 4096
DEPTH = 4

GRID_W = 64
CTX_LEN = 256
EPS = 1e-6
N_MOD = 6
ADA_RANK = 256
D_FF = 4 * D_MODEL
NA_DIM = 128
NA_WIDTH = D_MODEL // 4
NA_HEADS = NA_WIDTH // NA_DIM
NA_WIN_H = 8
NA_WIN_W = 16
GDN_DK = 128
GDN_DV = 128
GDN_WIDTH = 3 * D_MODEL // 8
GDN_HEADS = GDN_WIDTH // GDN_DV
GDN_CONV_CH = GDN_HEADS * (2 * GDN_DK + GDN_DV)
GDN_CHUNK = 64
SHORT_CONV = 5
DIFF_DIM = 64
DIFF_VDIM = 2 * DIFF_DIM
DIFF_WIDTH = D_MODEL - NA_WIDTH - GDN_WIDTH
DIFF_HEADS = DIFF_WIDTH // DIFF_VDIM
Q_BLOCK = 128
ROPE_BASE = 10000.0
MIX_WIDTH = NA_WIDTH + GDN_WIDTH + DIFF_WIDTH
IN_SPLITS = (NA_WIDTH, NA_WIDTH, NA_WIDTH, GDN_CONV_CH, GDN_WIDTH, 4 * GDN_HEADS,
             2 * DIFF_HEADS * DIFF_DIM, 2 * DIFF_HEADS * DIFF_DIM, DIFF_WIDTH)
N_IN = sum(IN_SPLITS)

kernel_name = 'hybrid_natten_gdn_diffattn_dit_block'


def rms_norm(x, w):
    xf = x.astype(jnp.float32)
    y = xf * lax.rsqrt(jnp.mean(xf * xf, axis=-1, keepdims=True) + EPS)
    return (y * w.astype(jnp.float32)).astype(x.dtype)


def l2_norm(x):
    return x * lax.rsqrt(jnp.sum(x * x, axis=-1, keepdims=True) + EPS)


def modulate(x, w, shift, scale):
    return rms_norm(x, w) * (1.0 + scale) + shift


def to_heads(t, n_heads):
    b, l, w = t.shape
    return t.reshape(b, l, n_heads, w // n_heads).transpose(0, 2, 1, 3)


def from_heads(t):
    b, h, l, d = t.shape
    return t.transpose(0, 2, 1, 3).reshape(b, l, h * d)


def split_cols(u):
    idx = [int(i) for i in np.cumsum(IN_SPLITS)[:-1]]
    return jnp.split(u, idx, axis=-1)


def axial_rope_tables(n_tokens, dim):
    n_freq = dim // 4
    inv = ROPE_BASE ** (-jnp.arange(n_freq, dtype=jnp.float32) / n_freq)
    t = jnp.arange(n_tokens)
    row = (t // GRID_W).astype(jnp.float32)
    col = (t % GRID_W).astype(jnp.float32)
    ang = jnp.concatenate([row[:, None] * inv, col[:, None] * inv], axis=-1)
    return jnp.cos(ang), jnp.sin(ang)


def apply_rope(x, cos, sin):
    x1, x2 = jnp.split(x, 2, axis=-1)
    cos = cos.astype(x.dtype)
    sin = sin.astype(x.dtype)
    return jnp.concatenate([x1 * cos - x2 * sin, x1 * sin + x2 * cos], axis=-1)


def softmax_attend(q, k, v):
    s = jnp.einsum('bhqd,bhkd->bhqk', q, k).astype(jnp.float32) * (q.shape[-1] ** -0.5)
    p = jax.nn.softmax(s, axis=-1).astype(v.dtype)
    return jnp.einsum('bhqk,bhkd->bhqd', p, v)


def neighbourhood_attention(q, k, v, kc, vc, rpb):
    b, h, s, d = q.shape
    rows = s // GRID_W
    kh = min(NA_WIN_H, rows)
    qg = q.reshape(b, h, rows, GRID_W, d)
    kg = k.reshape(b, h, rows, GRID_W, d)
    vg = v.reshape(b, h, rows, GRID_W, d)
    r = jnp.arange(rows)
    row_start = jnp.clip(r - kh // 2, 0, rows - kh)
    row_idx = row_start[:, None] + jnp.arange(kh)[None, :]
    k_band = kg[:, :, row_idx]
    v_band = vg[:, :, row_idx]
    cq = jnp.arange(GRID_W)
    col_start = jnp.clip(cq - NA_WIN_W // 2, 0, GRID_W - NA_WIN_W)
    col_ok = (cq[None, :] >= col_start[:, None]) & (cq[None, :] < col_start[:, None] + NA_WIN_W)
    dr = row_idx - r[:, None] + (NA_WIN_H - 1)
    dc = jnp.clip(cq[None, :] - cq[:, None], -(NA_WIN_W - 1), NA_WIN_W - 1) + (NA_WIN_W - 1)
    bias = rpb[:, dr[:, None, :, None], dc[None, :, None, :]].astype(jnp.float32)
    scale = d ** -0.5
    s_loc = jnp.einsum('bhrqd,bhrakd->bhrqak', qg, k_band).astype(jnp.float32) * scale + bias
    s_loc = jnp.where(col_ok[:, None, :], s_loc, -jnp.inf).reshape(b, h, rows, GRID_W, kh * GRID_W)
    s_ctx = jnp.einsum('bhrqd,bhcd->bhrqc', qg, kc).astype(jnp.float32) * scale
    p = jax.nn.softmax(jnp.concatenate([s_loc, s_ctx], axis=-1), axis=-1).astype(q.dtype)
    p_loc = p[..., :kh * GRID_W].reshape(b, h, rows, GRID_W, kh, GRID_W)
    p_ctx = p[..., kh * GRID_W:]
    o = jnp.einsum('bhrqak,bhrakd->bhrqd', p_loc, v_band) + jnp.einsum('bhrqc,bhcd->bhrqd', p_ctx, vc)
    return o.reshape(b, h, s, d)


def centred_depthwise_conv(x, w):
    k = w.shape[0]
    return lax.conv_general_dilated(x, w[:, None, :].astype(x.dtype), window_strides=(1,),
                                    padding=[(k // 2, k // 2)], dimension_numbers=('NWC', 'WIO', 'NWC'),
                                    feature_group_count=x.shape[-1])


def gdn_prepare(qkv, conv_w):
    qkv = jax.nn.silu(centred_depthwise_conv(qkv, conv_w))
    q, k, v = jnp.split(qkv, [GDN_HEADS * GDN_DK, 2 * GDN_HEADS * GDN_DK], axis=-1)
    q = l2_norm(to_heads(q, GDN_HEADS).astype(jnp.float32)) * (GDN_DK ** -0.5)
    k = l2_norm(to_heads(k, GDN_HEADS).astype(jnp.float32))
    v = to_heads(v, GDN_HEADS).astype(jnp.float32)
    return q, k, v


def gdn_gates(ab, a_log, dt_bias):
    b, l, _ = ab.shape
    ab = ab.astype(jnp.float32).reshape(b, l, 2, 2, GDN_HEADS)
    beta = jax.nn.sigmoid(ab[:, :, 0])
    g = -jnp.exp(a_log.astype(jnp.float32)) * jax.nn.softplus(ab[:, :, 1] + dt_bias.astype(jnp.float32))
    return jnp.transpose(g, (2, 0, 3, 1)), jnp.transpose(beta, (2, 0, 3, 1))


def gated_delta_chunked(q, k, v, g, beta, state, with_output):
    b, h, l, dk = q.shape
    dv = v.shape[-1]
    c = GDN_CHUNK
    n = l // c
    q = q.reshape(b, h, n, c, dk)
    k = k.reshape(b, h, n, c, dk)
    v = v.reshape(b, h, n, c, dv)
    g = jnp.cumsum(g.reshape(b, h, n, c), axis=-1)
    beta = beta.reshape(b, h, n, c)
    incl = jnp.tril(jnp.ones((c, c), bool))
    strict = jnp.tril(jnp.ones((c, c), bool), -1)
    decay = jnp.exp(jnp.where(incl, g[..., :, None] - g[..., None, :], -jnp.inf))
    kb = k * beta[..., None]
    lower = jnp.where(strict, jnp.einsum('bhnid,bhnjd->bhnij', kb, k) * decay, 0.0)
    eye = jnp.eye(c, dtype=jnp.float32)
    t_inv = lax.linalg.triangular_solve(lower + eye, jnp.broadcast_to(eye, lower.shape),
                                        left_side=True, lower=True, unit_diagonal=True)
    u = t_inv @ (v * beta[..., None])
    w = t_inv @ (kb * jnp.exp(g)[..., None])
    k_to_end = k * jnp.exp(g[..., -1:] - g)[..., None]
    chunk_decay = jnp.exp(g[..., -1])
    xs = [u, w, k_to_end, chunk_decay]
    if with_output:
        xs += [q * jnp.exp(g)[..., None], jnp.einsum('bhnid,bhnjd->bhnij', q, k) * decay]
    xs = tuple(jnp.moveaxis(t, 2, 0) for t in xs)

    def step(s, inp):
        u_i, w_i, kend_i, cd_i = inp[:4]
        v_new = u_i - w_i @ s
        s_next = s * cd_i[..., None, None] + jnp.einsum('bhcd,bhce->bhde', kend_i, v_new)
        if with_output:
            qd_i, a_i = inp[4:]
            return s_next, qd_i @ s + a_i @ v_new
        return s_next, None

    state, o = lax.scan(step, state, xs)
    if with_output:
        o = jnp.moveaxis(o, 0, 2).reshape(b, h, l, dv)
    return state, o


def gdn_bidirectional(ctx_qkv, lat_qkv, ctx_gates, lat_gates, with_ctx_out):
    qc, kc, vc = ctx_qkv
    ql, kl, vl = lat_qkv
    gc, bc = ctx_gates
    gl, bl = lat_gates
    b, h = ql.shape[:2]
    o_lat, o_ctx = [], []
    for d in range(2):
        f = (lambda t: jnp.flip(t, axis=2)) if d == 1 else (lambda t: t)
        s0 = jnp.zeros((b, h, GDN_DK, GDN_DV), jnp.float32)
        s_ctx, oc = gated_delta_chunked(f(qc), f(kc), f(vc), f(gc[d]), f(bc[d]), s0, with_ctx_out)
        _, ol = gated_delta_chunked(f(ql), f(kl), f(vl), f(gl[d]), f(bl[d]), s_ctx, True)
        o_lat.append(f(ol))
        if with_ctx_out:
            o_ctx.append(f(oc))
    return o_lat[0] + o_lat[1], (o_ctx[0] + o_ctx[1]) if with_ctx_out else None


def diff_attention_maps(q, k, v, lam):
    s = jnp.einsum('bhmqd,bhmkd->bhmqk', q, k).astype(jnp.float32) * (q.shape[-1] ** -0.5)
    p = jax.nn.softmax(s, axis=-1)
    a = (p[:, :, 0] - lam * p[:, :, 1]).astype(v.dtype)
    return jnp.einsum('bhqk,bhkd->bhqd', a, v)


def diff_attention_latent(q, k_all, v_all, lam):
    b, h, m, s, d = q.shape
    nb = s // Q_BLOCK
    qb = jnp.moveaxis(q.reshape(b, h, m, nb, Q_BLOCK, d), 3, 0)
    ob = lax.map(lambda qq: diff_attention_maps(qq, k_all, v_all, lam), qb)
    return jnp.moveaxis(ob, 0, 2).reshape(b, h, s, -1)


def na_group(lat, cx, p, last):
    qn, kn = p['na_q_norm'], p['na_k_norm']
    q = rms_norm(to_heads(lat[0], NA_HEADS), qn)
    k = rms_norm(to_heads(lat[1], NA_HEADS), kn)
    v = to_heads(lat[2], NA_HEADS)
    kc = rms_norm(to_heads(cx[1], NA_HEADS), kn)
    vc = to_heads(cx[2], NA_HEADS)
    o_lat = from_heads(neighbourhood_attention(q, k, v, kc, vc, p['na_rpb']))
    if last:
        return o_lat, None
    qc = rms_norm(to_heads(cx[0], NA_HEADS), qn)
    return o_lat, from_heads(softmax_attend(qc, kc, vc))


def gdn_group(lat, cx, p, last):
    lat_qkv = gdn_prepare(lat[3], p['gdn_conv'])
    ctx_qkv = gdn_prepare(cx[3], p['gdn_conv'])
    lat_gates = gdn_gates(lat[5], p['gdn_a_log'], p['gdn_dt_bias'])
    ctx_gates = gdn_gates(cx[5], p['gdn_a_log'], p['gdn_dt_bias'])
    o_lat, o_ctx = gdn_bidirectional(ctx_qkv, lat_qkv, ctx_gates, lat_gates, not last)
    out = lambda o, z: from_heads(rms_norm(o, p['gdn_norm'])).astype(z.dtype) * jax.nn.silu(z)
    return out(o_lat, lat[4]), (None if last else out(o_ctx, cx[4]))


def diff_group(lat, cx, p, cos, sin, lam_init, last):
    lv = p['diff_lambda'].astype(jnp.float32)
    lam = jnp.exp(jnp.dot(lv[0], lv[1])) - jnp.exp(jnp.dot(lv[2], lv[3])) + lam_init
    qn, kn = p['diff_q_norm'], p['diff_k_norm']

    def two_maps(t, w):
        b, l, _ = t.shape
        return rms_norm(t.reshape(b, l, DIFF_HEADS, 2, DIFF_DIM).transpose(0, 2, 3, 1, 4), w)

    q = apply_rope(two_maps(lat[6], qn), cos, sin)
    k = apply_rope(two_maps(lat[7], kn), cos, sin)
    v = to_heads(lat[8], DIFF_HEADS)
    kc = two_maps(cx[7], kn)
    vc = to_heads(cx[8], DIFF_HEADS)
    k_all = jnp.concatenate([kc, k], axis=3)
    v_all = jnp.concatenate([vc, v], axis=2)
    post = lambda o: from_heads(rms_norm(o, p['diff_subln']) * (1.0 - lam_init))
    o_lat = post(diff_attention_latent(q, k_all, v_all, lam))
    if last:
        return o_lat, None
    qc = two_maps(cx[6], qn)
    return o_lat, post(diff_attention_maps(qc, kc, vc, lam))


def sq_relu_mlp(h, w1, w2):
    return jnp.square(jax.nn.relu(h @ w1)) @ w2


def ada_mod(c_act, p):
    return (c_act @ p['ada_down']) @ p['ada_up'] + p['ada_bias']


def hybrid_layer(x, xc, c_act, cc_act, p, cos, sin, layer_idx, last):
    d = x.shape[-1]
    mod = ada_mod(c_act, p).reshape(-1, N_MOD, d)[:, :, None, :]
    modc = ada_mod(cc_act, p).reshape(N_MOD, d)
    h = modulate(x, p['norm1'], mod[:, 0], mod[:, 1])
    hc = modulate(xc, p['norm1'], modc[0], modc[1])
    lat = split_cols(h @ p['w_in'])
    cx = split_cols(hc @ p['w_in'])
    lam_init = 0.8 - 0.6 * math.exp(-0.3 * layer_idx)
    na_l, na_c = na_group(lat, cx, p, last)
    gdn_l, gdn_c = gdn_group(lat, cx, p, last)
    df_l, df_c = diff_group(lat, cx, p, cos, sin, lam_init, last)
    x = x + mod[:, 2] * (jnp.concatenate([na_l, gdn_l, df_l], axis=-1) @ p['w_out'])
    x = x + mod[:, 5] * sq_relu_mlp(modulate(x, p['norm2'], mod[:, 3], mod[:, 4]), p['w_ff1'], p['w_ff2'])
    if last:
        return x, None
    xc = xc + modc[2] * (jnp.concatenate([na_c, gdn_c, df_c], axis=-1) @ p['w_out'])
    xc = xc + modc[5] * sq_relu_mlp(modulate(xc, p['norm2'], modc[3], modc[4]), p['w_ff1'], p['w_ff2'])
    return x, xc


def setup_inputs(seed: int = 0) -> dict:
    key = jax.random.key(seed)
    ks = jax.random.split(key, 32)
    f32 = jnp.float32
    nrm = lambda i, shape, scale: jax.random.normal(ks[i], shape, f32) * scale
    gain = lambda i, shape: 1.0 + 0.02 * jax.random.normal(ks[i], shape, f32)
    dt = jnp.exp(jax.random.uniform(ks[20], (DEPTH, 2, GDN_HEADS), f32, math.log(1e-3), math.log(1e-1)))
    return {
        'x': nrm(0, (BATCH, SEQ, D_MODEL), 1.0),
        'c': nrm(1, (BATCH, D_MODEL), 1.0),
        'ctx': nrm(2, (BATCH, CTX_LEN, D_MODEL), 1.0),
        'c_ctx': nrm(3, (D_MODEL,), 1.0),
        'ada_down': nrm(4, (DEPTH, D_MODEL, ADA_RANK), D_MODEL ** -0.5),
        'ada_up': nrm(5, (DEPTH, ADA_RANK, N_MOD * D_MODEL), 0.5 * ADA_RANK ** -0.5),
        'ada_bias': nrm(6, (DEPTH, N_MOD * D_MODEL), 0.01),
        'norm1': gain(7, (DEPTH, D_MODEL)),
        'w_in': nrm(8, (DEPTH, D_MODEL, N_IN), D_MODEL ** -0.5),
        'na_q_norm': gain(9, (DEPTH, NA_DIM)),
        'na_k_norm': gain(10, (DEPTH, NA_DIM)),
        'na_rpb': nrm(11, (DEPTH, NA_HEADS, 2 * NA_WIN_H - 1, 2 * NA_WIN_W - 1), 0.2),
        'gdn_conv': nrm(12, (DEPTH, SHORT_CONV, GDN_CONV_CH), SHORT_CONV ** -0.5),
        'gdn_a_log': jnp.log(jax.random.uniform(ks[13], (DEPTH, 2, GDN_HEADS), f32, 1.0, 16.0)),
        'gdn_dt_bias': dt + jnp.log(-jnp.expm1(-dt)),
        'gdn_norm': gain(14, (DEPTH, GDN_DV)),
        'diff_q_norm': gain(15, (DEPTH, DIFF_DIM)),
        'diff_k_norm': gain(16, (DEPTH, DIFF_DIM)),
        'diff_lambda': nrm(17, (DEPTH, 4, DIFF_DIM), 0.1),
        'diff_subln': gain(18, (DEPTH, DIFF_VDIM)),
        'w_out': nrm(19, (DEPTH, MIX_WIDTH, D_MODEL), MIX_WIDTH ** -0.5),
        'norm2': gain(21, (DEPTH, D_MODEL)),
        'w_ff1': nrm(22, (DEPTH, D_MODEL, D_FF), D_MODEL ** -0.5),
        'w_ff2': nrm(23, (DEPTH, D_FF, D_MODEL), D_FF ** -0.5),
    }


def reference(x, c, ctx, c_ctx, ada_down, ada_up, ada_bias, norm1, w_in, na_q_norm, na_k_norm, na_rpb,
              gdn_conv, gdn_a_log, gdn_dt_bias, gdn_norm, diff_q_norm, diff_k_norm, diff_lambda, diff_subln,
              w_out, norm2, w_ff1, w_ff2):
    cos, sin = axial_rope_tables(x.shape[1], DIFF_DIM)
    c_act = jax.nn.silu(c)
    cc_act = jax.nn.silu(c_ctx)
    xc = ctx
    for l in range(DEPTH):
        p = {
            'ada_down': ada_down[l], 'ada_up': ada_up[l], 'ada_bias': ada_bias[l], 'norm1': norm1[l],
            'w_in': w_in[l], 'na_q_norm': na_q_norm[l], 'na_k_norm': na_k_norm[l], 'na_rpb': na_rpb[l],
            'gdn_conv': gdn_conv[l], 'gdn_a_log': gdn_a_log[l], 'gdn_dt_bias': gdn_dt_bias[l],
            'gdn_norm': gdn_norm[l], 'diff_q_norm': diff_q_norm[l], 'diff_k_norm': diff_k_norm[l],
            'diff_lambda': diff_lambda[l], 'diff_subln': diff_subln[l], 'w_out': w_out[l],
            'norm2': norm2[l], 'w_ff1': w_ff1[l], 'w_ff2': w_ff2[l],
        }
        x, xc = hybrid_layer(x, xc, c_act, cc_act, p, cos, sin, l, l == DEPTH - 1)
    return x
```

```python
import functools
import math

import jax
import jax.numpy as jnp
from jax import lax
from jax.experimental import pallas as pl
from jax.experimental.pallas import tpu as pltpu

GRID_W = 64
EPS = 1e-6
N_MOD = 6
HEAD = 128
NA_WIN_H = 8
NA_WIN_W = 16
GDN_CHUNK = 64
SHORT_CONV = 5
DIFF_DIM = 64
ROPE_BASE = 10000.0
NEG = -1e30

LANES = 128
ROW_BLOCK = 256
NA_ROWS = 8
VMEM_LIMIT = 56 * 1024 * 1024

F32 = jnp.float32
BF16 = jnp.bfloat16


def _params(sem, vmem=VMEM_LIMIT):
    return pltpu.CompilerParams(dimension_semantics=sem, vmem_limit_bytes=vmem)


def _pick_tile(n, target):
    best = None
    for t in range(LANES, min(n, target) + 1, LANES):
        if n % t == 0:
            best = t
    assert best is not None, (n, target)
    return best


def _dot_t(a, b):
    return lax.dot_general(a, b, (((1,), (1,)), ((), ())), preferred_element_type=F32)


def _rms(x, w):
    return x * lax.rsqrt(jnp.mean(x * x, axis=-1, keepdims=True) + EPS) * w


def _modulate_kernel(x_ref, w_ref, mod_ref, o_ref, *, shift_row):
    x = x_ref[...]
    y = _rms(x, w_ref[...])
    shift = mod_ref[0, shift_row:shift_row + 1, :]
    scale = mod_ref[0, shift_row + 1:shift_row + 2, :]
    o_ref[...] = (y * (1.0 + scale) + shift).astype(o_ref.dtype)


def _modulate(x, w, mod, shift_row, *, seq, n_batch):
    m, d = x.shape
    tr = ROW_BLOCK
    bps = seq // tr
    seg = lambda i: jnp.minimum(i // bps, n_batch)
    return pl.pallas_call(
        functools.partial(_modulate_kernel, shift_row=shift_row),
        out_shape=jax.ShapeDtypeStruct((m, d), BF16),
        grid=(m // tr,),
        in_specs=[pl.BlockSpec((tr, d), lambda i: (i, 0)),
                  pl.BlockSpec((1, d), lambda i: (0, 0)),
                  pl.BlockSpec((1, N_MOD, d), lambda i: (seg(i), 0, 0))],
        out_specs=pl.BlockSpec((tr, d), lambda i: (i, 0)),
        compiler_params=_params(("parallel",)),
        name="modulate",
    )(x, w.reshape(1, d), mod)


def _mm_kernel(*refs, nk, epilogue, gate_row):
    if epilogue == "gated_res":
        a_ref, w_ref, res_ref, mod_ref, o_ref = refs[:5]
        scratch = refs[5:]
    else:
        a_ref, w_ref, o_ref = refs[:3]
        scratch = refs[3:]

    def finish(acc):
        if epilogue == "relu2":
            r = jnp.maximum(acc, 0.0)
            acc = r * r
        elif epilogue == "gated_res":
            acc = res_ref[...] + mod_ref[0, gate_row:gate_row + 1, :] * acc
        o_ref[...] = acc.astype(o_ref.dtype)

    part = jnp.dot(a_ref[...], w_ref[...], preferred_element_type=F32)
    if nk == 1:
        finish(part)
    else:
        acc_ref = scratch[0]
        k = pl.program_id(2)

        @pl.when(k == 0)
        def _():
            acc_ref[...] = part

        @pl.when(k > 0)
        def _():
            acc_ref[...] += part

        @pl.when(k == nk - 1)
        def _():
            finish(acc_ref[...])


def _matmul(a, w, *, tm, tn, tk, out_dtype, epilogue="none", res=None, mod=None, gate_row=0,
            seq=None, n_batch=None, name="matmul"):
    m, kdim = a.shape
    n = w.shape[1]
    assert m % tm == 0 and n % tn == 0 and kdim % tk == 0, (m, n, kdim, tm, tn, tk)
    nk = kdim // tk
    in_specs = [pl.BlockSpec((tm, tk), lambda j, i, k: (i, k)),
                pl.BlockSpec((tk, tn), lambda j, i, k: (k, j))]
    args = [a, w]
    if epilogue == "gated_res":
        bps = seq // tm
        in_specs += [pl.BlockSpec((tm, tn), lambda j, i, k: (i, j)),
                     pl.BlockSpec((1, N_MOD, tn), lambda j, i, k: (jnp.minimum(i // bps, n_batch), 0, j))]
        args += [res, mod]
    return pl.pallas_call(
        functools.partial(_mm_kernel, nk=nk, epilogue=epilogue, gate_row=gate_row),
        out_shape=jax.ShapeDtypeStruct((m, n), out_dtype),
        grid=(n // tn, m // tm, nk),
        in_specs=in_specs,
        out_specs=pl.BlockSpec((tm, tn), lambda j, i, k: (i, j)),
        scratch_shapes=[pltpu.VMEM((tm, tn), F32)] if nk > 1 else [],
        compiler_params=_params(("parallel", "parallel", "arbitrary")),
        name=name,
    )(*args)


def _na_kernel(q_ref, kp_ref, kc_ref, kn_ref, vp_ref, vc_ref, vn_ref, kx_ref, vx_ref,
               bias_ref, qw_ref, kw_ref, o_ref):
    scale = HEAD ** -0.5
    qw = qw_ref[...]
    kw = kw_ref[...]
    q = (_rms(q_ref[...], qw) * scale).astype(BF16)
    kl = jnp.concatenate([_rms(kp_ref[...], kw), _rms(kc_ref[...], kw), _rms(kn_ref[...], kw)],
                         axis=0).astype(BF16)
    kx = _rms(kx_ref[...], kw).astype(BF16)
    s_loc = _dot_t(q, kl) + bias_ref[0, 0]
    s_ctx = _dot_t(q, kx)
    mx = jnp.maximum(jnp.max(s_loc, axis=-1, keepdims=True), jnp.max(s_ctx, axis=-1, keepdims=True))
    e_loc = jnp.exp(s_loc - mx)
    e_ctx = jnp.exp(s_ctx - mx)
    denom = jnp.sum(e_loc, axis=-1, keepdims=True) + jnp.sum(e_ctx, axis=-1, keepdims=True)
    vl = jnp.concatenate([vp_ref[...], vc_ref[...], vn_ref[...]], axis=0).astype(BF16)
    o = jnp.dot(e_loc.astype(BF16), vl, preferred_element_type=F32)
    o = o + jnp.dot(e_ctx.astype(BF16), vx_ref[...].astype(BF16), preferred_element_type=F32)
    o_ref[...] = (o / denom).astype(o_ref.dtype)


def _na_bias_tables(rpb):
    h = rpb.shape[0]
    cq = jnp.arange(GRID_W)
    dc = jnp.clip(cq[None, :] - cq[:, None], -(NA_WIN_W - 1), NA_WIN_W - 1) + (NA_WIN_W - 1)
    col_start = jnp.clip(cq - NA_WIN_W // 2, 0, GRID_W - NA_WIN_W)
    col_ok = (cq[None, :] >= col_start[:, None]) & (cq[None, :] < col_start[:, None] + NA_WIN_W)
    t = jnp.where(col_ok, rpb[:, :, dc], NEG)
    t = jnp.concatenate([t, jnp.full((h, 1, GRID_W, GRID_W), NEG, F32)], axis=1)
    rq = jnp.arange(NA_ROWS)[:, None]
    rk = jnp.arange(3 * NA_ROWS)[None, :]
    dr = rk - rq - 1
    half = NA_WIN_H // 2
    starts = (NA_ROWS + jnp.maximum(rq - half, 0), NA_ROWS + rq - half, NA_ROWS + jnp.minimum(rq - half, 0))
    tabs = []
    for start in starts:
        valid = (rk >= start) & (rk < start + NA_WIN_H)
        idx = jnp.where(valid, dr, 2 * NA_WIN_H - 1)
        g = t[:, idx]
        tabs.append(g.transpose(0, 1, 3, 2, 4).reshape(h, NA_ROWS * GRID_W, 3 * NA_ROWS * GRID_W))
    return jnp.stack(tabs)


def _na_latent(u, bias, qw, kw, *, n_batch, seq, ctx_len, n_heads):
    tq = NA_ROWS * GRID_W
    nj = seq // tq
    assert nj >= 2 and seq % tq == 0
    kcol = n_heads
    vcol = 2 * n_heads
    ctx_blk = n_batch * seq // ctx_len
    qmap = lambda b, h, j: (b * nj + j, h)
    prev = lambda col: (lambda b, h, j: (b * nj + jnp.maximum(j - 1, 0), col + h))
    cur = lambda col: (lambda b, h, j: (b * nj + j, col + h))
    nxt = lambda col: (lambda b, h, j: (b * nj + jnp.minimum(j + 1, nj - 1), col + h))
    cx = lambda col: (lambda b, h, j: (ctx_blk + b, col + h))
    variant = lambda j: jnp.where(j == 0, 0, jnp.where(j == nj - 1, 2, 1))
    blk = lambda rows, imap: pl.BlockSpec((rows, HEAD), imap)
    return pl.pallas_call(
        _na_kernel,
        out_shape=jax.ShapeDtypeStruct((n_batch * seq, n_heads * HEAD), BF16),
        grid=(n_batch, n_heads, nj),
        in_specs=[blk(tq, qmap),
                  blk(tq, prev(kcol)), blk(tq, cur(kcol)), blk(tq, nxt(kcol)),
                  blk(tq, prev(vcol)), blk(tq, cur(vcol)), blk(tq, nxt(vcol)),
                  blk(ctx_len, cx(kcol)), blk(ctx_len, cx(vcol)),
                  pl.BlockSpec((1, 1, tq, 3 * tq), lambda b, h, j: (variant(j), h, 0, 0)),
                  pl.BlockSpec((1, HEAD), lambda b, h, j: (0, 0)),
                  pl.BlockSpec((1, HEAD), lambda b, h, j: (0, 0))],
        out_specs=blk(tq, qmap),
        compiler_params=_params(("parallel", "parallel", "arbitrary")),
        name="na_latent",
    )(u, u, u, u, u, u, u, u, u, bias, qw.reshape(1, HEAD), kw.reshape(1, HEAD))


def _na_ctx_kernel(q_ref, k_ref, v_ref, qw_ref, kw_ref, o_ref):
    q = (_rms(q_ref[...], qw_ref[...]) * (HEAD ** -0.5)).astype(BF16)
    k = _rms(k_ref[...], kw_ref[...]).astype(BF16)
    s = _dot_t(q, k)
    e = jnp.exp(s - jnp.max(s, axis=-1, keepdims=True))
    o = jnp.dot(e.astype(BF16), v_ref[...].astype(BF16), preferred_element_type=F32)
    o_ref[...] = (o / jnp.sum(e, axis=-1, keepdims=True)).astype(o_ref.dtype)


def _na_context(u, qw, kw, *, n_batch, seq, ctx_len, n_heads):
    ctx_blk = n_batch * seq // ctx_len
    cx = lambda col: (lambda b, h: (ctx_blk + b, col + h))
    blk = lambda imap: pl.BlockSpec((ctx_len, HEAD), imap)
    return pl.pallas_call(
        _na_ctx_kernel,
        out_shape=jax.ShapeDtypeStruct((n_batch * ctx_len, n_heads * HEAD), BF16),
        grid=(n_batch, n_heads),
        in_specs=[blk(cx(0)), blk(cx(n_heads)), blk(cx(2 * n_heads)),
                  pl.BlockSpec((1, HEAD), lambda b, h: (0, 0)),
                  pl.BlockSpec((1, HEAD), lambda b, h: (0, 0))],
        out_specs=blk(lambda b, h: (b, h)),
        compiler_params=_params(("parallel", "parallel")),
        name="na_context",
    )(u, u, u, qw.reshape(1, HEAD), kw.reshape(1, HEAD))


def _gdn_conv_kernel(prev_ref, cur_ref, next_ref, w_ref, o_ref, *, lat_blocks, seq_blocks, ctx_blocks, n_heads):
    i = pl.program_id(0)
    j = pl.program_id(1)
    pos = jnp.where(i < lat_blocks, i % seq_blocks, (i - lat_blocks) % ctx_blocks)
    length = jnp.where(i < lat_blocks, seq_blocks, ctx_blocks)
    first = (pos == 0).astype(F32)
    last = (pos == length - 1).astype(F32)
    tr = cur_ref.shape[0]
    ext = jnp.concatenate([prev_ref[...] * (1.0 - first), cur_ref[...], next_ref[...] * (1.0 - last)], axis=0)
    n_ext = ext.shape[0]
    acc = jnp.zeros((tr, ext.shape[1]), F32)
    for tap in range(SHORT_CONV):
        d = tap - SHORT_CONV // 2
        shifted = ext if d == 0 else pltpu.roll(ext, (-d) % n_ext, axis=0)
        acc = acc + shifted[8:8 + tr] * w_ref[tap:tap + 1, :]
    y = acc * (1.0 / (1.0 + jnp.exp(-acc)))

    @pl.when(j < 2)
    def _():
        scale = jnp.where(j == 0, HEAD ** -0.5, 1.0)
        for h in range(n_heads):
            t = y[:, h * HEAD:(h + 1) * HEAD]
            t = t * (lax.rsqrt(jnp.sum(t * t, axis=-1, keepdims=True) + EPS) * scale)
            o_ref[:, h * HEAD:(h + 1) * HEAD] = t

    @pl.when(j == 2)
    def _():
        o_ref[...] = y


def _gdn_conv(u, conv_w, *, n_batch, seq, ctx_len, gw):
    m = u.shape[0]
    tr = ROW_BLOCK
    n_heads = gw // HEAD
    lat_blocks = n_batch * seq // tr
    halo = tr // 8
    nblk8 = m // 8
    return pl.pallas_call(
        functools.partial(_gdn_conv_kernel, lat_blocks=lat_blocks, seq_blocks=seq // tr,
                          ctx_blocks=ctx_len // tr, n_heads=n_heads),
        out_shape=jax.ShapeDtypeStruct((m, 3 * gw), F32),
        grid=(m // tr, 3),
        in_specs=[pl.BlockSpec((8, gw), lambda i, j: (jnp.maximum(i * halo - 1, 0), 2 + j)),
                  pl.BlockSpec((tr, gw), lambda i, j: (i, 2 + j)),
                  pl.BlockSpec((8, gw), lambda i, j: (jnp.minimum((i + 1) * halo, nblk8 - 1), 2 + j)),
                  pl.BlockSpec((SHORT_CONV, gw), lambda i, j: (0, j))],
        out_specs=pl.BlockSpec((tr, gw), lambda i, j: (i, j)),
        compiler_params=_params(("parallel", "parallel")),
        name="gdn_conv",
    )(u, u, u, conv_w)


def _cumsum_rows(x, reverse):
    n = x.shape[0]
    row = lax.broadcasted_iota(jnp.int32, x.shape, 0)
    s = 1
    while s < n:
        if reverse:
            x = x + jnp.where(row < n - s, pltpu.roll(x, n - s, axis=0), 0.0)
        else:
            x = x + jnp.where(row >= s, pltpu.roll(x, s, axis=0), 0.0)
        s *= 2
    return x


def _dot_split(a, b):
    a_hi = a.astype(BF16)
    b_hi = b.astype(BF16)
    a_lo = (a - a_hi.astype(F32)).astype(BF16)
    b_lo = (b - b_hi.astype(F32)).astype(BF16)
    dot = functools.partial(jnp.dot, preferred_element_type=F32)
    return dot(a_hi, b_hi) + dot(a_hi, b_lo) + dot(a_lo, b_hi)


def _unit_triangular_inverse(lmat, ij_xor):
    c = lmat.shape[0]
    eye = (ij_xor == 0).astype(F32)
    x = eye - jnp.where(ij_xor < 2, lmat, 0.0)
    blk = 2
    while blk < c:
        coupling = jnp.where((ij_xor // blk) == 1, lmat, 0.0)
        x = x - _dot_split(_dot_split(x, coupling), x)
        blk *= 2
    return x


def _gdn_chain(q, k, v, beta, gc_col, gc_row, g_last, state, reverse):
    c = q.shape[0]
    ii = lax.broadcasted_iota(jnp.int32, (c, c), 0)
    jj = lax.broadcasted_iota(jnp.int32, (c, c), 1)
    incl = (jj >= ii) if reverse else (jj <= ii)
    strict = (jj > ii) if reverse else (jj < ii)
    decay = jnp.exp(jnp.where(incl, gc_col - gc_row, NEG))
    eg = jnp.exp(gc_col)
    kb = k * beta
    k16 = k.astype(BF16)
    kk = _dot_t(jnp.concatenate([kb, q], axis=0).astype(BF16), k16)
    lmat = jnp.where(strict, kk[:c] * decay, 0.0)
    amat = kk[c:] * decay
    tinv = _unit_triangular_inverse(lmat, ii ^ jj)
    rhs = jnp.concatenate([v * beta, kb * eg], axis=1).astype(BF16)
    uw = jnp.dot(tinv.astype(BF16), rhs, preferred_element_type=F32)
    dv = v.shape[1]
    u_mat = uw[:, :dv]
    w_mat = uw[:, dv:]
    s16 = state.astype(BF16)
    ws = jnp.dot(jnp.concatenate([w_mat, q * eg], axis=0).astype(BF16), s16, preferred_element_type=F32)
    v_new = u_mat - ws[:c]
    v16 = v_new.astype(BF16)
    out = ws[c:] + jnp.dot(amat.astype(BF16), v16, preferred_element_type=F32)
    k_end_t = (k * jnp.exp(g_last - gc_col)).T.astype(BF16)
    new_state = state * jnp.exp(g_last) + jnp.dot(k_end_t, v16, preferred_element_type=F32)
    return out, new_state


def _gdn_scan_kernel(*refs, n_group, n_heads):
    (qf_ref, kf_ref, vf_ref, gf_ref, qb_ref, kb_ref, vb_ref, gb_ref, a_ref, dt_ref,
     of_ref, ob_ref, state_ref) = refs
    t = pl.program_id(2)
    hg = pl.program_id(1)

    @pl.when(t == 0)
    def _():
        state_ref[...] = jnp.zeros_like(state_ref)

    c = GDN_CHUNK
    lane = lax.broadcasted_iota(jnp.int32, (c, LANES), 1)
    for d, (q_ref, k_ref, v_ref, g_ref, o_ref) in enumerate(
            ((qf_ref, kf_ref, vf_ref, gf_ref, of_ref), (qb_ref, kb_ref, vb_ref, gb_ref, ob_ref))):
        reverse = d == 1
        raw = g_ref[...]
        beta_all = 1.0 / (1.0 + jnp.exp(-raw))
        z = raw + dt_ref[...]
        softplus = jnp.maximum(z, 0.0) + jnp.log1p(jnp.exp(-jnp.abs(z)))
        gcum = _cumsum_rows(a_ref[...] * softplus, reverse)
        gcum_t = gcum.T
        last_row = 0 if reverse else c - 1
        for g in range(n_group):
            head = hg * n_group + g
            bcol = d * n_heads + head
            gcol = 2 * n_heads + bcol
            beta = jnp.sum(jnp.where(lane == bcol, beta_all, 0.0), axis=1, keepdims=True)
            gc_col = jnp.sum(jnp.where(lane == gcol, gcum, 0.0), axis=1, keepdims=True)
            sub = lax.broadcasted_iota(jnp.int32, gcum_t.shape, 0)
            gc_row = jnp.sum(jnp.where(sub == gcol, gcum_t, 0.0), axis=0, keepdims=True)
            g_last = gc_col[last_row:last_row + 1, :]
            sl = slice(g * HEAD, (g + 1) * HEAD)
            out, new_state = _gdn_chain(q_ref[:, sl], k_ref[:, sl], v_ref[:, sl], beta, gc_col, gc_row,
                                        g_last, state_ref[d * n_group + g], reverse)
            o_ref[:, sl] = out
            state_ref[d * n_group + g] = new_state


def _gdn_scan(qkv, graw, a_row, dt_row, *, n_batch, seq, ctx_len, gw):
    m = qkv.shape[0]
    c = GDN_CHUNK
    n_heads = gw // HEAD
    n_group = max(g for g in (4, 3, 2, 1) if n_heads % g == 0)
    n_hg = n_heads // n_group
    n_ctx = ctx_len // c
    n_lat = seq // c
    gcw = n_group * HEAD

    def row_block(b, t, d):
        tl = t - n_ctx
        c_ctx = t if d == 0 else n_ctx - 1 - t
        c_lat = tl if d == 0 else n_lat - 1 - tl
        return jnp.where(t < n_ctx, (n_batch * seq + b * ctx_len) // c + c_ctx, (b * seq) // c + c_lat)

    def spec(col0, d):
        return pl.BlockSpec((c, gcw), lambda b, hg, t: (row_block(b, t, d), col0 + hg))

    def gspec(d):
        return pl.BlockSpec((c, LANES), lambda b, hg, t: (row_block(b, t, d), 0))

    def ospec(d):
        return pl.BlockSpec((c, gcw), lambda b, hg, t: (row_block(b, t, d), hg))

    const = pl.BlockSpec((1, LANES), lambda b, hg, t: (0, 0))
    in_specs = []
    for d in range(2):
        in_specs += [spec(0, d), spec(n_hg, d), spec(2 * n_hg, d), gspec(d)]
    in_specs += [const, const]
    args = [qkv, qkv, qkv, graw, qkv, qkv, qkv, graw, a_row, dt_row]
    return pl.pallas_call(
        functools.partial(_gdn_scan_kernel, n_group=n_group, n_heads=n_heads),
        out_shape=[jax.ShapeDtypeStruct((m, gw), F32), jax.ShapeDtypeStruct((m, gw), F32)],
        grid=(n_batch, n_hg, n_ctx + n_lat),
        in_specs=in_specs,
        out_specs=[ospec(0), ospec(1)],
        scratch_shapes=[pltpu.VMEM((2 * n_group, HEAD, HEAD), F32)],
        compiler_params=_params(("parallel", "parallel", "arbitrary")),
        name="gdn_scan",
    )(*args)


def _gdn_out_kernel(of_ref, ob_ref, z_ref, w_ref, o_ref, *, n_heads):
    w = w_ref[...]
    for h in range(n_heads):
        sl = slice(h * HEAD, (h + 1) * HEAD)
        o = _rms(of_ref[:, sl] + ob_ref[:, sl], w)
        z = z_ref[:, sl]
        o_ref[:, sl] = (o * (z / (1.0 + jnp.exp(-z)))).astype(o_ref.dtype)


def _gdn_out(o_fwd, o_bwd, u, w, *, gw):
    m = o_fwd.shape[0]
    tr = ROW_BLOCK
    blk = lambda col: pl.BlockSpec((tr, gw), lambda i: (i, col))
    return pl.pallas_call(
        functools.partial(_gdn_out_kernel, n_heads=gw // HEAD),
        out_shape=jax.ShapeDtypeStruct((m, gw), BF16),
        grid=(m // tr,),
        in_specs=[blk(0), blk(0), blk(5), pl.BlockSpec((1, HEAD), lambda i: (0, 0))],
        out_specs=blk(0),
        compiler_params=_params(("parallel",)),
        name="gdn_out",
    )(o_fwd, o_bwd, u, w.reshape(1, HEAD))


def _diff_prep_kernel(x_ref, w_ref, cos_ref, sin_ref, o_ref, *, n_heads):
    j = pl.program_id(1)

    @pl.when(j < 2)
    def _():
        w = w_ref[0]
        cos = cos_ref[...]
        sin = sin_ref[...]
        lane = lax.broadcasted_iota(jnp.int32, (x_ref.shape[0], HEAD), 1)
        low = lane < DIFF_DIM
        first_half = (lane % DIFF_DIM) < DIFF_DIM // 2
        scale = jnp.where(j == 0, DIFF_DIM ** -0.5, 1.0)
        for h in range(n_heads):
            sl = slice(h * HEAD, (h + 1) * HEAD)
            x = x_ref[:, sl]
            sq = x * x
            s_low = jnp.sum(jnp.where(low, sq, 0.0), axis=-1, keepdims=True)
            s_high = jnp.sum(jnp.where(low, 0.0, sq), axis=-1, keepdims=True)
            ms = jnp.where(low, s_low, s_high) * (1.0 / DIFF_DIM)
            y = x * lax.rsqrt(ms + EPS) * w
            swapped = jnp.where(first_half, pltpu.roll(y, HEAD - DIFF_DIM // 2, axis=1),
                                pltpu.roll(y, DIFF_DIM // 2, axis=1))
            o_ref[:, sl] = ((y * cos + swapped * sin) * scale).astype(o_ref.dtype)

    @pl.when(j == 2)
    def _():
        o_ref[...] = x_ref[...].astype(o_ref.dtype)


def _diff_prep(u, w_qk, cos_t, sin_t, *, n_batch, seq, gw):
    m = u.shape[0]
    tr = ROW_BLOCK
    lat_blocks = n_batch * seq // tr
    seq_blocks = seq // tr
    rope_map = lambda i, j: (jnp.where(i < lat_blocks, i % seq_blocks, seq_blocks), 0)
    return pl.pallas_call(
        functools.partial(_diff_prep_kernel, n_heads=gw // HEAD),
        out_shape=jax.ShapeDtypeStruct((m, 3 * gw), BF16),
        grid=(m // tr, 3),
        in_specs=[pl.BlockSpec((tr, gw), lambda i, j: (i, 6 + j)),
                  pl.BlockSpec((1, 1, HEAD), lambda i, j: (jnp.minimum(j, 1), 0, 0)),
                  pl.BlockSpec((tr, HEAD), rope_map),
                  pl.BlockSpec((tr, HEAD), rope_map)],
        out_specs=pl.BlockSpec((tr, gw), lambda i, j: (i, j)),
        compiler_params=_params(("parallel", "parallel")),
        name="diff_prep",
    )(u, w_qk, cos_t, sin_t)


def _diff_attn_kernel(*refs, n_seg):
    q_ref = refs[0]
    k_refs = refs[1:1 + n_seg]
    v_refs = refs[1 + n_seg:1 + 2 * n_seg]
    lam_ref, w_ref, o_ref = refs[1 + 2 * n_seg:]
    tq = q_ref.shape[0]
    q = q_ref[...]
    lane = lax.broadcasted_iota(jnp.int32, q.shape, 1)
    zero = jnp.zeros_like(q)
    q2 = jnp.concatenate([jnp.where(lane < DIFF_DIM, q, zero), jnp.where(lane < DIFF_DIM, zero, q)], axis=0)
    scores = [_dot_t(q2, k_ref[...]) for k_ref in k_refs]
    mx = functools.reduce(jnp.maximum, [jnp.max(s, axis=-1, keepdims=True) for s in scores])
    es = [jnp.exp(s - mx) for s in scores]
    denom = functools.reduce(jnp.add, [jnp.sum(e, axis=-1, keepdims=True) for e in es])
    inv = 1.0 / denom
    c0 = inv[:tq]
    c1 = inv[tq:] * lam_ref[0:1, 0:1]
    o = None
    for e, v_ref in zip(es, v_refs):
        a = (e[:tq] * c0 - e[tq:] * c1).astype(BF16)
        part = jnp.dot(a, v_ref[...], preferred_element_type=F32)
        o = part if o is None else o + part
    o_ref[...] = _rms(o, w_ref[...]).astype(o_ref.dtype)


def _diff_attention(dp, lam_row, w_sub, *, n_batch, seq, ctx_len, n_heads, latent):
    kcol = n_heads
    vcol = 2 * n_heads
    ctx_blk = n_batch * seq // ctx_len
    if latent:
        tq = min(seq, 256)
        nq = seq // tq
        qmap = lambda b, h, i: (b * nq + i, h)
        segs = [(seq, lambda col: (lambda b, h, i: (b, col + h))),
                (ctx_len, lambda col: (lambda b, h, i: (ctx_blk + b, col + h)))]
        out_rows = n_batch * seq
        omap = qmap
    else:
        tq = ctx_len
        nq = 1
        qmap = lambda b, h, i: (ctx_blk + b, h)
        segs = [(ctx_len, lambda col: (lambda b, h, i: (ctx_blk + b, col + h)))]
        out_rows = n_batch * ctx_len
        omap = lambda b, h, i: (b, h)
    in_specs = [pl.BlockSpec((tq, HEAD), qmap)]
    in_specs += [pl.BlockSpec((rows, HEAD), mk(kcol)) for rows, mk in segs]
    in_specs += [pl.BlockSpec((rows, HEAD), mk(vcol)) for rows, mk in segs]
    in_specs += [pl.BlockSpec((1, HEAD), lambda b, h, i: (0, 0))] * 2
    n_seg = len(segs)
    return pl.pallas_call(
        functools.partial(_diff_attn_kernel, n_seg=n_seg),
        out_shape=jax.ShapeDtypeStruct((out_rows, n_heads * HEAD), BF16),
        grid=(n_batch, n_heads, nq),
        in_specs=in_specs,
        out_specs=pl.BlockSpec((tq, HEAD), omap),
        compiler_params=_params(("parallel", "parallel", "arbitrary")),
        name="diff_attn_latent" if latent else "diff_attn_context",
    )(*([dp] * (1 + 2 * n_seg)), lam_row, w_sub)


def _rope_tables(seq, extra_rows):
    n_freq = DIFF_DIM // 4
    inv = ROPE_BASE ** (-jnp.arange(n_freq, dtype=F32) / n_freq)
    t = jnp.arange(seq)
    row = (t // GRID_W).astype(F32)
    col = (t % GRID_W).astype(F32)
    ang = jnp.concatenate([row[:, None] * inv, col[:, None] * inv], axis=-1)
    cos, sin = jnp.cos(ang), jnp.sin(ang)
    cos_t = jnp.tile(cos, (1, HEAD // cos.shape[1]))
    sin_t = jnp.tile(jnp.concatenate([-sin, sin], axis=-1), (1, HEAD // (2 * sin.shape[1])))
    cos_t = jnp.concatenate([cos_t, jnp.ones((extra_rows, HEAD), F32)], axis=0)
    sin_t = jnp.concatenate([sin_t, jnp.zeros((extra_rows, HEAD), F32)], axis=0)
    return cos_t, sin_t


def _kernel_impl(x, c, ctx, c_ctx, ada_down, ada_up, ada_bias, norm1, w_in, na_q_norm, na_k_norm, na_rpb,
                 gdn_conv, gdn_a_log, gdn_dt_bias, gdn_norm, diff_q_norm, diff_k_norm, diff_lambda, diff_subln,
                 w_out, norm2, w_ff1, w_ff2):
    n_batch, seq, d = x.shape
    ctx_len = ctx.shape[1]
    depth = w_in.shape[0]
    d_ff = w_ff1.shape[2]
    naw = d // 4
    gw = 3 * d // 8
    na_heads = naw // HEAD
    g_heads = gw // HEAD
    n_gate = 4 * g_heads
    m_lat = n_batch * seq
    m_all = m_lat + n_batch * ctx_len
    n_main = 9 * gw
    gate_off = 3 * naw + 4 * gw
    assert w_in.shape[2] == n_main + n_gate and n_gate <= LANES
    assert seq % (NA_ROWS * GRID_W) == 0 and ctx_len % ROW_BLOCK == 0 and seq % ctx_len == 0

    xs = jnp.concatenate([x.reshape(m_lat, d), ctx.reshape(n_batch * ctx_len, d)], axis=0)
    cond = jnp.concatenate([c, c_ctx[None, :]], axis=0)
    cond = cond * (1.0 / (1.0 + jnp.exp(-cond)))
    cond = jnp.pad(cond, ((0, 16 - (n_batch + 1)), (0, 0))).astype(BF16)
    cos_t, sin_t = _rope_tables(seq, ROW_BLOCK)
    dims = dict(n_batch=n_batch, seq=seq, ctx_len=ctx_len)
    tm = 512

    for l in range(depth):
        last = l == depth - 1
        lam_init = 0.8 - 0.6 * math.exp(-0.3 * l)
        w_main = jnp.concatenate([w_in[l][:, :gate_off], w_in[l][:, gate_off + n_gate:]], axis=1).astype(BF16)
        w_gate = jnp.pad(w_in[l][:, gate_off:gate_off + n_gate], ((0, 0), (0, LANES - n_gate))).astype(BF16)
        a_row = jnp.pad(-jnp.exp(gdn_a_log[l]).reshape(1, -1), ((0, 0), (2 * g_heads, LANES - n_gate)))
        dt_row = jnp.pad(gdn_dt_bias[l].reshape(1, -1), ((0, 0), (2 * g_heads, LANES - n_gate)))
        lv = diff_lambda[l]
        lam = jnp.exp(jnp.sum(lv[0] * lv[1])) - jnp.exp(jnp.sum(lv[2] * lv[3])) + lam_init
        lam_row = jnp.full((1, HEAD), lam, F32)
        w_sub = (diff_subln[l] * (1.0 - lam_init)).reshape(1, HEAD)
        w_qk = jnp.stack([jnp.tile(diff_q_norm[l], 2), jnp.tile(diff_k_norm[l], 2)]).reshape(2, 1, HEAD)
        bias = _na_bias_tables(na_rpb[l])

        low = _matmul(cond, ada_down[l].astype(BF16), tm=16, tn=ada_down.shape[2], tk=d, out_dtype=BF16,
                      name="ada_down")
        mod = _matmul(low, ada_up[l].astype(BF16), tm=16, tn=_pick_tile(N_MOD * d, 4096), tk=ada_down.shape[2],
                      out_dtype=F32, name="ada_up")
        mod = (mod[:n_batch + 1] + ada_bias[l][None, :]).reshape(n_batch + 1, N_MOD, d)

        h1 = _modulate(xs, norm1[l], mod, 0, seq=seq, n_batch=n_batch)
        u = _matmul(h1, w_main, tm=tm, tn=_pick_tile(n_main, 1536), tk=d, out_dtype=F32, name="w_in")
        graw = _matmul(h1, w_gate, tm=tm, tn=LANES, tk=d, out_dtype=F32, name="w_in_gates")

        na_lat = _na_latent(u, bias, na_q_norm[l], na_k_norm[l], n_heads=na_heads, **dims)
        qkv = _gdn_conv(u, gdn_conv[l], gw=gw, **dims)
        o_fwd, o_bwd = _gdn_scan(qkv, graw, a_row, dt_row, gw=gw, **dims)
        gdn = _gdn_out(o_fwd, o_bwd, u, gdn_norm[l], gw=gw)
        dp = _diff_prep(u, w_qk, cos_t, sin_t, n_batch=n_batch, seq=seq, gw=gw)
        df_lat = _diff_attention(dp, lam_row, w_sub, n_heads=g_heads, latent=True, **dims)

        if last:
            mix = jnp.concatenate([na_lat, gdn[:m_lat], df_lat], axis=1)
            xs = xs[:m_lat]
        else:
            na_ctx = _na_context(u, na_q_norm[l], na_k_norm[l], n_heads=na_heads, **dims)
            df_ctx = _diff_attention(dp, lam_row, w_sub, n_heads=g_heads, latent=False, **dims)
            mix = jnp.concatenate([jnp.concatenate([na_lat, na_ctx], axis=0), gdn,
                                   jnp.concatenate([df_lat, df_ctx], axis=0)], axis=1)

        xs = _matmul(mix, w_out[l].astype(BF16), tm=tm, tn=_pick_tile(d, 1024), tk=d, out_dtype=F32,
                     epilogue="gated_res", res=xs, mod=mod, gate_row=2, seq=seq, n_batch=n_batch, name="w_out")
        h2 = _modulate(xs, norm2[l], mod, 3, seq=seq, n_batch=n_batch)
        ff = _matmul(h2, w_ff1[l].astype(BF16), tm=tm, tn=_pick_tile(d_ff, 1024), tk=d, out_dtype=BF16,
                     epilogue="relu2", name="w_ff1")
        xs = _matmul(ff, w_ff2[l].astype(BF16), tm=tm, tn=_pick_tile(d, 1024), tk=min(d_ff, 2048), out_dtype=F32,
                     epilogue="gated_res", res=xs, mod=mod, gate_row=5, seq=seq, n_batch=n_batch, name="w_ff2")

    return xs[:m_lat].reshape(n_batch, seq, d)


kernel = jax.jit(_kernel_impl)
```

```python
import functools
import math

import jax
import jax.numpy as jnp
from jax import lax
from jax.experimental import pallas as pl
from jax.experimental.pallas import tpu as pltpu

GRID_W = 64
EPS = 1e-6
N_MOD = 6
HEAD = 128
NA_WIN_H = 8
NA_WIN_W = 16
GDN_CHUNK = 64
SHORT_CONV = 5
DIFF_DIM = 64
ROPE_BASE = 10000.0
NEG = -1e30

LANES = 128
ROW_BLOCK = 256
NA_ROWS = 8
DIFF_Q_ROWS = 512
DIFF_SUB_ROWS = 128
VMEM_LIMIT = 56 * 1024 * 1024

F32 = jnp.float32
BF16 = jnp.bfloat16


def _params(sem, vmem=VMEM_LIMIT):
    return pltpu.CompilerParams(dimension_semantics=sem, vmem_limit_bytes=vmem)


def _pick_tile(n, target):
    best = None
    for t in range(LANES, min(n, target) + 1, LANES):
        if n % t == 0:
            best = t
    assert best is not None, (n, target)
    return best


def _dot_t(a, b):
    return lax.dot_general(a, b, (((1,), (1,)), ((), ())), preferred_element_type=F32)


def _rms(x, w):
    return x * lax.rsqrt(jnp.mean(x * x, axis=-1, keepdims=True) + EPS) * w


def _modulate_kernel(x_ref, w_ref, mod_ref, o_ref, *, shift_row):
    x = x_ref[...]
    y = _rms(x, w_ref[...])
    shift = mod_ref[0, shift_row:shift_row + 1, :]
    scale = mod_ref[0, shift_row + 1:shift_row + 2, :]
    o_ref[...] = (y * (1.0 + scale) + shift).astype(o_ref.dtype)


def _modulate(x, w, mod, shift_row, *, seq, n_batch):
    m, d = x.shape
    tr = ROW_BLOCK
    bps = seq // tr
    seg = lambda i: jnp.minimum(i // bps, n_batch)
    return pl.pallas_call(
        functools.partial(_modulate_kernel, shift_row=shift_row),
        out_shape=jax.ShapeDtypeStruct((m, d), BF16),
        grid=(m // tr,),
        in_specs=[pl.BlockSpec((tr, d), lambda i: (i, 0)),
                  pl.BlockSpec((1, d), lambda i: (0, 0)),
                  pl.BlockSpec((1, N_MOD, d), lambda i: (seg(i), 0, 0))],
        out_specs=pl.BlockSpec((tr, d), lambda i: (i, 0)),
        compiler_params=_params(("parallel",)),
        name="modulate",
    )(x, w.reshape(1, d), mod)


def _mm_kernel(*refs, nk, epilogue, gate_row):
    if epilogue == "gated_res":
        a_ref, w_ref, res_ref, mod_ref, o_ref = refs[:5]
        scratch = refs[5:]
    else:
        a_ref, w_ref, o_ref = refs[:3]
        scratch = refs[3:]

    def finish(acc):
        if epilogue == "relu2":
            r = jnp.maximum(acc, 0.0)
            acc = r * r
        elif epilogue == "gated_res":
            acc = res_ref[...] + mod_ref[0, gate_row:gate_row + 1, :] * acc
        o_ref[...] = acc.astype(o_ref.dtype)

    part = jnp.dot(a_ref[...], w_ref[...], preferred_element_type=F32)
    if nk == 1:
        finish(part)
    else:
        acc_ref = scratch[0]
        k = pl.program_id(2)

        @pl.when(k == 0)
        def _():
            acc_ref[...] = part

        @pl.when(k > 0)
        def _():
            acc_ref[...] += part

        @pl.when(k == nk - 1)
        def _():
            finish(acc_ref[...])


def _matmul(a, w, *, tm, tn, tk, out_dtype, epilogue="none", res=None, mod=None, gate_row=0,
            seq=None, n_batch=None, name="matmul"):
    m, kdim = a.shape
    n = w.shape[1]
    assert m % tm == 0 and n % tn == 0 and kdim % tk == 0, (m, n, kdim, tm, tn, tk)
    nk = kdim // tk
    in_specs = [pl.BlockSpec((tm, tk), lambda j, i, k: (i, k)),
                pl.BlockSpec((tk, tn), lambda j, i, k: (k, j))]
    args = [a, w]
    if epilogue == "gated_res":
        bps = seq // tm
        in_specs += [pl.BlockSpec((tm, tn), lambda j, i, k: (i, j)),
                     pl.BlockSpec((1, N_MOD, tn), lambda j, i, k: (jnp.minimum(i // bps, n_batch), 0, j))]
        args += [res, mod]
    return pl.pallas_call(
        functools.partial(_mm_kernel, nk=nk, epilogue=epilogue, gate_row=gate_row),
        out_shape=jax.ShapeDtypeStruct((m, n), out_dtype),
        grid=(n // tn, m // tm, nk),
        in_specs=in_specs,
        out_specs=pl.BlockSpec((tm, tn), lambda j, i, k: (i, j)),
        scratch_shapes=[pltpu.VMEM((tm, tn), F32)] if nk > 1 else [],
        compiler_params=_params(("parallel", "parallel", "arbitrary")),
        name=name,
    )(*args)


def _na_kernel(q_ref, kp_ref, kc_ref, kn_ref, vp_ref, vc_ref, vn_ref, kx_ref, vx_ref,
               bias_ref, qw_ref, kw_ref, o_ref):
    scale = HEAD ** -0.5
    qw = qw_ref[...]
    kw = kw_ref[...]
    q = (_rms(q_ref[...], qw) * scale).astype(BF16)
    kl = jnp.concatenate([_rms(kp_ref[...], kw), _rms(kc_ref[...], kw), _rms(kn_ref[...], kw)],
                         axis=0).astype(BF16)
    kx = _rms(kx_ref[...], kw).astype(BF16)
    s_loc = _dot_t(q, kl) + bias_ref[0, 0]
    s_ctx = _dot_t(q, kx)
    mx = jnp.maximum(jnp.max(s_loc, axis=-1, keepdims=True), jnp.max(s_ctx, axis=-1, keepdims=True))
    e_loc = jnp.exp(s_loc - mx)
    e_ctx = jnp.exp(s_ctx - mx)
    denom = jnp.sum(e_loc, axis=-1, keepdims=True) + jnp.sum(e_ctx, axis=-1, keepdims=True)
    vl = jnp.concatenate([vp_ref[...], vc_ref[...], vn_ref[...]], axis=0).astype(BF16)
    o = jnp.dot(e_loc.astype(BF16), vl, preferred_element_type=F32)
    o = o + jnp.dot(e_ctx.astype(BF16), vx_ref[...].astype(BF16), preferred_element_type=F32)
    o_ref[...] = (o / denom).astype(o_ref.dtype)


def _na_bias_tables(rpb):
    h = rpb.shape[0]
    cq = jnp.arange(GRID_W)
    dc = jnp.clip(cq[None, :] - cq[:, None], -(NA_WIN_W - 1), NA_WIN_W - 1) + (NA_WIN_W - 1)
    col_start = jnp.clip(cq - NA_WIN_W // 2, 0, GRID_W - NA_WIN_W)
    col_ok = (cq[None, :] >= col_start[:, None]) & (cq[None, :] < col_start[:, None] + NA_WIN_W)
    t = jnp.where(col_ok, rpb[:, :, dc], NEG)
    t = jnp.concatenate([t, jnp.full((h, 1, GRID_W, GRID_W), NEG, F32)], axis=1)
    rq = jnp.arange(NA_ROWS)[:, None]
    rk = jnp.arange(3 * NA_ROWS)[None, :]
    dr = rk - rq - 1
    half = NA_WIN_H // 2
    starts = (NA_ROWS + jnp.maximum(rq - half, 0), NA_ROWS + rq - half, NA_ROWS + jnp.minimum(rq - half, 0))
    tabs = []
    for start in starts:
        valid = (rk >= start) & (rk < start + NA_WIN_H)
        idx = jnp.where(valid, dr, 2 * NA_WIN_H - 1)
        g = t[:, idx]
        tabs.append(g.transpose(0, 1, 3, 2, 4).reshape(h, NA_ROWS * GRID_W, 3 * NA_ROWS * GRID_W))
    return jnp.stack(tabs)


def _na_latent(u, bias, qw, kw, *, n_batch, seq, ctx_len, n_heads):
    tq = NA_ROWS * GRID_W
    nj = seq // tq
    assert nj >= 2 and seq % tq == 0
    kcol = n_heads
    vcol = 2 * n_heads
    ctx_blk = n_batch * seq // ctx_len
    qmap = lambda b, h, j: (b * nj + j, h)
    prev = lambda col: (lambda b, h, j: (b * nj + jnp.maximum(j - 1, 0), col + h))
    cur = lambda col: (lambda b, h, j: (b * nj + j, col + h))
    nxt = lambda col: (lambda b, h, j: (b * nj + jnp.minimum(j + 1, nj - 1), col + h))
    cx = lambda col: (lambda b, h, j: (ctx_blk + b, col + h))
    variant = lambda j: jnp.where(j == 0, 0, jnp.where(j == nj - 1, 2, 1))
    blk = lambda rows, imap: pl.BlockSpec((rows, HEAD), imap)
    return pl.pallas_call(
        _na_kernel,
        out_shape=jax.ShapeDtypeStruct((n_batch * seq, n_heads * HEAD), BF16),
        grid=(n_batch, n_heads, nj),
        in_specs=[blk(tq, qmap),
                  blk(tq, prev(kcol)), blk(tq, cur(kcol)), blk(tq, nxt(kcol)),
                  blk(tq, prev(vcol)), blk(tq, cur(vcol)), blk(tq, nxt(vcol)),
                  blk(ctx_len, cx(kcol)), blk(ctx_len, cx(vcol)),
                  pl.BlockSpec((1, 1, tq, 3 * tq), lambda b, h, j: (variant(j), h, 0, 0)),
                  pl.BlockSpec((1, HEAD), lambda b, h, j: (0, 0)),
                  pl.BlockSpec((1, HEAD), lambda b, h, j: (0, 0))],
        out_specs=blk(tq, qmap),
        compiler_params=_params(("parallel", "parallel", "arbitrary")),
        name="na_latent",
    )(u, u, u, u, u, u, u, u, u, bias, qw.reshape(1, HEAD), kw.reshape(1, HEAD))


def _na_ctx_kernel(q_ref, k_ref, v_ref, qw_ref, kw_ref, o_ref):
    q = (_rms(q_ref[...], qw_ref[...]) * (HEAD ** -0.5)).astype(BF16)
    k = _rms(k_ref[...], kw_ref[...]).astype(BF16)
    s = _dot_t(q, k)
    e = jnp.exp(s - jnp.max(s, axis=-1, keepdims=True))
    o = jnp.dot(e.astype(BF16), v_ref[...].astype(BF16), preferred_element_type=F32)
    o_ref[...] = (o / jnp.sum(e, axis=-1, keepdims=True)).astype(o_ref.dtype)


def _na_context(u, qw, kw, *, n_batch, seq, ctx_len, n_heads):
    ctx_blk = n_batch * seq // ctx_len
    cx = lambda col: (lambda b, h: (ctx_blk + b, col + h))
    blk = lambda imap: pl.BlockSpec((ctx_len, HEAD), imap)
    return pl.pallas_call(
        _na_ctx_kernel,
        out_shape=jax.ShapeDtypeStruct((n_batch * ctx_len, n_heads * HEAD), BF16),
        grid=(n_batch, n_heads),
        in_specs=[blk(cx(0)), blk(cx(n_heads)), blk(cx(2 * n_heads)),
                  pl.BlockSpec((1, HEAD), lambda b, h: (0, 0)),
                  pl.BlockSpec((1, HEAD), lambda b, h: (0, 0))],
        out_specs=blk(lambda b, h: (b, h)),
        compiler_params=_params(("parallel", "parallel")),
        name="na_context",
    )(u, u, u, qw.reshape(1, HEAD), kw.reshape(1, HEAD))


def _gdn_conv_kernel(prev_ref, cur_ref, next_ref, w_ref, o_ref, *, lat_blocks, seq_blocks, ctx_blocks, n_heads):
    i = pl.program_id(0)
    j = pl.program_id(1)
    pos = jnp.where(i < lat_blocks, i % seq_blocks, (i - lat_blocks) % ctx_blocks)
    length = jnp.where(i < lat_blocks, seq_blocks, ctx_blocks)
    first = (pos == 0).astype(F32)
    last = (pos == length - 1).astype(F32)
    tr = cur_ref.shape[0]
    ext = jnp.concatenate([prev_ref[...] * (1.0 - first), cur_ref[...], next_ref[...] * (1.0 - last)], axis=0)
    n_ext = ext.shape[0]
    acc = jnp.zeros((tr, ext.shape[1]), F32)
    for tap in range(SHORT_CONV):
        d = tap - SHORT_CONV // 2
        shifted = ext if d == 0 else pltpu.roll(ext, (-d) % n_ext, axis=0)
        acc = acc + shifted[8:8 + tr] * w_ref[tap:tap + 1, :]
    y = acc * (1.0 / (1.0 + jnp.exp(-acc)))

    @pl.when(j < 2)
    def _():
        scale = jnp.where(j == 0, HEAD ** -0.5, 1.0)
        for h in range(n_heads):
            t = y[:, h * HEAD:(h + 1) * HEAD]
            t = t * (lax.rsqrt(jnp.sum(t * t, axis=-1, keepdims=True) + EPS) * scale)
            o_ref[:, h * HEAD:(h + 1) * HEAD] = t

    @pl.when(j == 2)
    def _():
        o_ref[...] = y


def _gdn_conv(u, conv_w, *, n_batch, seq, ctx_len, gw):
    m = u.shape[0]
    tr = ROW_BLOCK
    n_heads = gw // HEAD
    lat_blocks = n_batch * seq // tr
    halo = tr // 8
    nblk8 = m // 8
    return pl.pallas_call(
        functools.partial(_gdn_conv_kernel, lat_blocks=lat_blocks, seq_blocks=seq // tr,
                          ctx_blocks=ctx_len // tr, n_heads=n_heads),
        out_shape=jax.ShapeDtypeStruct((m, 3 * gw), F32),
        grid=(m // tr, 3),
        in_specs=[pl.BlockSpec((8, gw), lambda i, j: (jnp.maximum(i * halo - 1, 0), 2 + j)),
                  pl.BlockSpec((tr, gw), lambda i, j: (i, 2 + j)),
                  pl.BlockSpec((8, gw), lambda i, j: (jnp.minimum((i + 1) * halo, nblk8 - 1), 2 + j)),
                  pl.BlockSpec((SHORT_CONV, gw), lambda i, j: (0, j))],
        out_specs=pl.BlockSpec((tr, gw), lambda i, j: (i, j)),
        compiler_params=_params(("parallel", "parallel")),
        name="gdn_conv",
    )(u, u, u, conv_w)


def _cumsum_rows(x, reverse):
    n = x.shape[0]
    row = lax.broadcasted_iota(jnp.int32, x.shape, 0)
    s = 1
    while s < n:
        if reverse:
            x = x + jnp.where(row < n - s, pltpu.roll(x, n - s, axis=0), 0.0)
        else:
            x = x + jnp.where(row >= s, pltpu.roll(x, s, axis=0), 0.0)
        s *= 2
    return x


def _unit_triangular_inverse(lmats, ij_xor):
    c = lmats[0].shape[0]
    eye = (ij_xor == 0).astype(F32)
    dot = functools.partial(jnp.dot, preferred_element_type=F32)
    xs = [eye - jnp.where(ij_xor < 2, lm, 0.0) for lm in lmats]
    blk = 2
    while blk < c:
        pick = (ij_xor // blk) == 1
        x16 = [x.astype(BF16) for x in xs]
        xc = [dot(x, jnp.where(pick, lm, 0.0).astype(BF16)) for x, lm in zip(x16, lmats)]
        xcx = [dot(y.astype(BF16), x) for y, x in zip(xc, x16)]
        xs = [x - z for x, z in zip(xs, xcx)]
        blk *= 2
    return xs


def _gdn_scan_kernel(qf_ref, kf_ref, vf_ref, gf_ref, qb_ref, kb_ref, vb_ref, gb_ref, a_ref, dt_ref,
                     of_ref, ob_ref, state_ref, *, n_heads):
    @pl.when(pl.program_id(1) == 0)
    def _():
        state_ref[...] = jnp.zeros_like(state_ref)

    c = GDN_CHUNK
    dot = functools.partial(jnp.dot, preferred_element_type=F32)
    ii = lax.broadcasted_iota(jnp.int32, (c, c), 0)
    jj = lax.broadcasted_iota(jnp.int32, (c, c), 1)
    ij_xor = ii ^ jj

    qs, ks, vs, betas, gcols, grows, egs, kes, cds, incls, stricts, outs = ([] for _ in range(12))
    for d, (q_ref, k_ref, v_ref, g_ref, o_ref) in enumerate(
            ((qf_ref, kf_ref, vf_ref, gf_ref, of_ref), (qb_ref, kb_ref, vb_ref, gb_ref, ob_ref))):
        reverse = d == 1
        raw = g_ref[...]
        beta_all = 1.0 / (1.0 + jnp.exp(-raw))
        z = raw + dt_ref[...]
        softplus = jnp.maximum(z, 0.0) + jnp.log1p(jnp.exp(-jnp.abs(z)))
        gcum = _cumsum_rows(a_ref[...] * softplus, reverse)
        gcum_t = gcum.T
        last_row = 0 if reverse else c - 1
        g_last = gcum[last_row:last_row + 1, :]
        eg_all = jnp.exp(gcum)
        ke_all = jnp.exp(g_last - gcum)
        cd_all = jnp.exp(g_last)
        incl = (jj >= ii) if reverse else (jj <= ii)
        strict = (jj > ii) if reverse else (jj < ii)
        for h in range(n_heads):
            bcol = d * n_heads + h
            gcol = 2 * n_heads + bcol
            sl = slice(h * HEAD, (h + 1) * HEAD)
            qs.append(q_ref[:, sl])
            ks.append(k_ref[:, sl])
            vs.append(v_ref[:, sl])
            betas.append(beta_all[:, bcol:bcol + 1])
            gcols.append(gcum[:, gcol:gcol + 1])
            grows.append(gcum_t[gcol:gcol + 1, :])
            egs.append(eg_all[:, gcol:gcol + 1])
            kes.append(ke_all[:, gcol:gcol + 1])
            cds.append(cd_all[:, gcol:gcol + 1])
            incls.append(incl)
            stricts.append(strict)
            outs.append((o_ref, sl))
    n = len(qs)
    rng = range(n)

    decay = [jnp.exp(jnp.where(incls[i], gcols[i] - grows[i], NEG)) for i in rng]
    kbs = [ks[i] * betas[i] for i in rng]
    kk = [_dot_t(jnp.concatenate([kbs[i], qs[i]], axis=0).astype(BF16), ks[i].astype(BF16)) for i in rng]
    lmats = [jnp.where(stricts[i], kk[i][:c] * decay[i], 0.0) for i in rng]
    amats = [(kk[i][c:] * decay[i]).astype(BF16) for i in rng]
    tinvs = _unit_triangular_inverse(lmats, ij_xor)
    rhs = [jnp.concatenate([vs[i] * betas[i], kbs[i] * egs[i]], axis=1).astype(BF16) for i in rng]
    uw = [dot(tinvs[i].astype(BF16), rhs[i]).astype(BF16) for i in rng]
    auw = [dot(amats[i], uw[i]) for i in rng]
    kuw = [lax.dot_general((ks[i] * kes[i]).astype(BF16), uw[i], (((0,), (0,)), ((), ())),
                           preferred_element_type=F32) for i in rng]
    q2 = [(qs[i] * egs[i] - auw[i][:, HEAD:]).astype(BF16) for i in rng]
    states = [state_ref[i] for i in rng]
    s16 = [st.astype(BF16) for st in states]
    out = [auw[i][:, :HEAD] + dot(q2[i], s16[i]) for i in rng]
    new_states = [states[i] * cds[i] + kuw[i][:, :HEAD] - dot(kuw[i][:, HEAD:].astype(BF16), s16[i]) for i in rng]
    for i in rng:
        o_ref, sl = outs[i]
        o_ref[:, sl] = out[i]
        state_ref[i] = new_states[i]


def _gdn_scan(qkv, graw, a_row, dt_row, *, n_batch, seq, ctx_len, gw):
    m = qkv.shape[0]
    c = GDN_CHUNK
    n_heads = gw // HEAD
    n_ctx = ctx_len // c
    n_lat = seq // c

    def row_block(b, t, d):
        tl = t - n_ctx
        c_ctx = t if d == 0 else n_ctx - 1 - t
        c_lat = tl if d == 0 else n_lat - 1 - tl
        return jnp.where(t < n_ctx, (n_batch * seq + b * ctx_len) // c + c_ctx, (b * seq) // c + c_lat)

    def spec(col, d):
        return pl.BlockSpec((c, gw), lambda b, t: (row_block(b, t, d), col))

    def gspec(d):
        return pl.BlockSpec((c, LANES), lambda b, t: (row_block(b, t, d), 0))

    const = pl.BlockSpec((1, LANES), lambda b, t: (0, 0))
    in_specs = []
    for d in range(2):
        in_specs += [spec(0, d), spec(1, d), spec(2, d), gspec(d)]
    in_specs += [const, const]
    args = [qkv, qkv, qkv, graw, qkv, qkv, qkv, graw, a_row, dt_row]
    return pl.pallas_call(
        functools.partial(_gdn_scan_kernel, n_heads=n_heads),
        out_shape=[jax.ShapeDtypeStruct((m, gw), F32), jax.ShapeDtypeStruct((m, gw), F32)],
        grid=(n_batch, n_ctx + n_lat),
        in_specs=in_specs,
        out_specs=[spec(0, 0), spec(0, 1)],
        scratch_shapes=[pltpu.VMEM((2 * n_heads, HEAD, HEAD), F32)],
        compiler_params=_params(("parallel", "arbitrary")),
        name="gdn_scan",
    )(*args)


def _gdn_out_kernel(of_ref, ob_ref, z_ref, w_ref, o_ref, *, n_heads):
    w = w_ref[...]
    for h in range(n_heads):
        sl = slice(h * HEAD, (h + 1) * HEAD)
        o = _rms(of_ref[:, sl] + ob_ref[:, sl], w)
        z = z_ref[:, sl]
        o_ref[:, sl] = (o * (z / (1.0 + jnp.exp(-z)))).astype(o_ref.dtype)


def _gdn_out(o_fwd, o_bwd, u, w, *, gw):
    m = o_fwd.shape[0]
    tr = ROW_BLOCK
    blk = lambda col: pl.BlockSpec((tr, gw), lambda i: (i, col))
    return pl.pallas_call(
        functools.partial(_gdn_out_kernel, n_heads=gw // HEAD),
        out_shape=jax.ShapeDtypeStruct((m, gw), BF16),
        grid=(m // tr,),
        in_specs=[blk(0), blk(0), blk(5), pl.BlockSpec((1, HEAD), lambda i: (0, 0))],
        out_specs=blk(0),
        compiler_params=_params(("parallel",)),
        name="gdn_out",
    )(o_fwd, o_bwd, u, w.reshape(1, HEAD))


def _diff_prep_kernel(x_ref, w_ref, cos_ref, sin_ref, o_ref, *, n_heads):
    j = pl.program_id(1)

    @pl.when(j < 2)
    def _():
        w = w_ref[0]
        cos = cos_ref[...]
        sin = sin_ref[...]
        lane = lax.broadcasted_iota(jnp.int32, (x_ref.shape[0], HEAD), 1)
        low = lane < DIFF_DIM
        first_half = (lane % DIFF_DIM) < DIFF_DIM // 2
        scale = jnp.where(j == 0, DIFF_DIM ** -0.5 * math.log2(math.e), 1.0)
        for h in range(n_heads):
            sl = slice(h * HEAD, (h + 1) * HEAD)
            x = x_ref[:, sl]
            sq = x * x
            s_low = jnp.sum(jnp.where(low, sq, 0.0), axis=-1, keepdims=True)
            s_high = jnp.sum(jnp.where(low, 0.0, sq), axis=-1, keepdims=True)
            ms = jnp.where(low, s_low, s_high) * (1.0 / DIFF_DIM)
            y = x * lax.rsqrt(ms + EPS) * w
            swapped = jnp.where(first_half, pltpu.roll(y, HEAD - DIFF_DIM // 2, axis=1),
                                pltpu.roll(y, DIFF_DIM // 2, axis=1))
            o_ref[:, sl] = ((y * cos + swapped * sin) * scale).astype(o_ref.dtype)

    @pl.when(j == 2)
    def _():
        o_ref[...] = x_ref[...].astype(o_ref.dtype)


def _diff_prep(u, w_qk, cos_t, sin_t, *, n_batch, seq, gw):
    m = u.shape[0]
    tr = ROW_BLOCK
    lat_blocks = n_batch * seq // tr
    seq_blocks = seq // tr
    rope_map = lambda i, j: (jnp.where(i < lat_blocks, i % seq_blocks, seq_blocks), 0)
    return pl.pallas_call(
        functools.partial(_diff_prep_kernel, n_heads=gw // HEAD),
        out_shape=jax.ShapeDtypeStruct((m, 3 * gw), BF16),
        grid=(m // tr, 3),
        in_specs=[pl.BlockSpec((tr, gw), lambda i, j: (i, 6 + j)),
                  pl.BlockSpec((1, 1, HEAD), lambda i, j: (jnp.minimum(j, 1), 0, 0)),
                  pl.BlockSpec((tr, HEAD), rope_map),
                  pl.BlockSpec((tr, HEAD), rope_map)],
        out_specs=pl.BlockSpec((tr, gw), lambda i, j: (i, j)),
        compiler_params=_params(("parallel", "parallel")),
        name="diff_prep",
    )(u, w_qk, cos_t, sin_t)


def _diff_attn_kernel(*refs, n_seg):
    q_ref = refs[0]
    k_refs = refs[1:1 + n_seg]
    v_refs = refs[1 + n_seg:1 + 2 * n_seg]
    lam_ref, w_ref, o_ref = refs[1 + 2 * n_seg:]
    tq = q_ref.shape[0]
    ts = min(tq, DIFF_SUB_ROWS)
    subs = range(tq // ts)
    lane = lax.broadcasted_iota(jnp.int32, (ts, HEAD), 1)
    zero = jnp.zeros((ts, HEAD), q_ref.dtype)
    v1s = [jnp.concatenate([v_ref[...], jnp.ones(v_ref.shape, v_ref.dtype)], axis=1) for v_ref in v_refs]
    q2s = []
    for i in subs:
        q = q_ref[i * ts:(i + 1) * ts, :]
        q2s.append(jnp.concatenate([jnp.where(lane < DIFF_DIM, q, zero), jnp.where(lane < DIFF_DIM, zero, q)], axis=0))
    scores = [[_dot_t(q2, k_ref[...]) for k_ref in k_refs] for q2 in q2s]
    mxs = [functools.reduce(jnp.maximum, [jnp.max(s, axis=-1, keepdims=True) for s in sc]) for sc in scores]
    es = [[jnp.exp2((s - mx).astype(BF16)) for s in sc] for sc, mx in zip(scores, mxs)]
    accs = [functools.reduce(jnp.add, [jnp.dot(e, v1, preferred_element_type=F32) for e, v1 in zip(e_seg, v1s)])
            for e_seg in es]
    for i, acc in zip(subs, accs):
        r = acc[:, :HEAD] / acc[:, HEAD:HEAD + 1]
        o = r[:ts] - lam_ref[0:1, 0:1] * r[ts:]
        o_ref[i * ts:(i + 1) * ts, :] = _rms(o, w_ref[...]).astype(o_ref.dtype)


def _diff_attention(dp, lam_row, w_sub, *, n_batch, seq, ctx_len, n_heads, latent):
    kcol = n_heads
    vcol = 2 * n_heads
    ctx_blk = n_batch * seq // ctx_len
    if latent:
        tq = min(seq, DIFF_Q_ROWS)
        nq = seq // tq
        qmap = lambda b, h, i: (b * nq + i, h)
        segs = [(seq, lambda col: (lambda b, h, i: (b, col + h))),
                (ctx_len, lambda col: (lambda b, h, i: (ctx_blk + b, col + h)))]
        out_rows = n_batch * seq
        omap = qmap
    else:
        tq = ctx_len
        nq = 1
        qmap = lambda b, h, i: (ctx_blk + b, h)
        segs = [(ctx_len, lambda col: (lambda b, h, i: (ctx_blk + b, col + h)))]
        out_rows = n_batch * ctx_len
        omap = lambda b, h, i: (b, h)
    in_specs = [pl.BlockSpec((tq, HEAD), qmap)]
    in_specs += [pl.BlockSpec((rows, HEAD), mk(kcol)) for rows, mk in segs]
    in_specs += [pl.BlockSpec((rows, HEAD), mk(vcol)) for rows, mk in segs]
    in_specs += [pl.BlockSpec((1, HEAD), lambda b, h, i: (0, 0))] * 2
    n_seg = len(segs)
    return pl.pallas_call(
        functools.partial(_diff_attn_kernel, n_seg=n_seg),
        out_shape=jax.ShapeDtypeStruct((out_rows, n_heads * HEAD), BF16),
        grid=(n_batch, n_heads, nq),
        in_specs=in_specs,
        out_specs=pl.BlockSpec((tq, HEAD), omap),
        compiler_params=_params(("parallel", "parallel", "arbitrary")),
        name="diff_attn_latent" if latent else "diff_attn_context",
    )(*([dp] * (1 + 2 * n_seg)), lam_row, w_sub)


def _rope_tables(seq, extra_rows):
    n_freq = DIFF_DIM // 4
    inv = ROPE_BASE ** (-jnp.arange(n_freq, dtype=F32) / n_freq)
    t = jnp.arange(seq)
    row = (t // GRID_W).astype(F32)
    col = (t % GRID_W).astype(F32)
    ang = jnp.concatenate([row[:, None] * inv, col[:, None] * inv], axis=-1)
    cos, sin = jnp.cos(ang), jnp.sin(ang)
    cos_t = jnp.tile(cos, (1, HEAD // cos.shape[1]))
    sin_t = jnp.tile(jnp.concatenate([-sin, sin], axis=-1), (1, HEAD // (2 * sin.shape[1])))
    cos_t = jnp.concatenate([cos_t, jnp.ones((extra_rows, HEAD), F32)], axis=0)
    sin_t = jnp.concatenate([sin_t, jnp.zeros((extra_rows, HEAD), F32)], axis=0)
    return cos_t, sin_t


def _kernel_impl(x, c, ctx, c_ctx, ada_down, ada_up, ada_bias, norm1, w_in, na_q_norm, na_k_norm, na_rpb,
                 gdn_conv, gdn_a_log, gdn_dt_bias, gdn_norm, diff_q_norm, diff_k_norm, diff_lambda, diff_subln,
                 w_out, norm2, w_ff1, w_ff2):
    n_batch, seq, d = x.shape
    ctx_len = ctx.shape[1]
    depth = w_in.shape[0]
    d_ff = w_ff1.shape[2]
    naw = d // 4
    gw = 3 * d // 8
    na_heads = naw // HEAD
    g_heads = gw // HEAD
    n_gate = 4 * g_heads
    m_lat = n_batch * seq
    m_all = m_lat + n_batch * ctx_len
    n_main = 9 * gw
    gate_off = 3 * naw + 4 * gw
    assert w_in.shape[2] == n_main + n_gate and n_gate <= LANES
    assert seq % (NA_ROWS * GRID_W) == 0 and ctx_len % ROW_BLOCK == 0 and seq % ctx_len == 0

    xs = jnp.concatenate([x.reshape(m_lat, d), ctx.reshape(n_batch * ctx_len, d)], axis=0)
    cond = jnp.concatenate([c, c_ctx[None, :]], axis=0)
    cond = cond * (1.0 / (1.0 + jnp.exp(-cond)))
    cond = jnp.pad(cond, ((0, 16 - (n_batch + 1)), (0, 0))).astype(BF16)
    cos_t, sin_t = _rope_tables(seq, ROW_BLOCK)
    dims = dict(n_batch=n_batch, seq=seq, ctx_len=ctx_len)
    tm = 512

    for l in range(depth):
        last = l == depth - 1
        lam_init = 0.8 - 0.6 * math.exp(-0.3 * l)
        w_main = jnp.concatenate([w_in[l][:, :gate_off], w_in[l][:, gate_off + n_gate:]], axis=1).astype(BF16)
        w_gate = jnp.pad(w_in[l][:, gate_off:gate_off + n_gate], ((0, 0), (0, LANES - n_gate))).astype(BF16)
        a_row = jnp.pad(-jnp.exp(gdn_a_log[l]).reshape(1, -1), ((0, 0), (2 * g_heads, LANES - n_gate)))
        dt_row = jnp.pad(gdn_dt_bias[l].reshape(1, -1), ((0, 0), (2 * g_heads, LANES - n_gate)))
        lv = diff_lambda[l]
        lam = jnp.exp(jnp.sum(lv[0] * lv[1])) - jnp.exp(jnp.sum(lv[2] * lv[3])) + lam_init
        lam_row = jnp.full((1, HEAD), lam, F32)
        w_sub = (diff_subln[l] * (1.0 - lam_init)).reshape(1, HEAD)
        w_qk = jnp.stack([jnp.tile(diff_q_norm[l], 2), jnp.tile(diff_k_norm[l], 2)]).reshape(2, 1, HEAD)
        bias = _na_bias_tables(na_rpb[l])

        low = _matmul(cond, ada_down[l].astype(BF16), tm=16, tn=ada_down.shape[2], tk=d, out_dtype=BF16,
                      name="ada_down")
        mod = _matmul(low, ada_up[l].astype(BF16), tm=16, tn=_pick_tile(N_MOD * d, 4096), tk=ada_down.shape[2],
                      out_dtype=F32, name="ada_up")
        mod = (mod[:n_batch + 1] + ada_bias[l][None, :]).reshape(n_batch + 1, N_MOD, d)

        h1 = _modulate(xs, norm1[l], mod, 0, seq=seq, n_batch=n_batch)
        u = _matmul(h1, w_main, tm=tm, tn=_pick_tile(n_main, 1536), tk=d, out_dtype=F32, name="w_in")
        graw = _matmul(h1, w_gate, tm=tm, tn=LANES, tk=d, out_dtype=F32, name="w_in_gates")

        na_lat = _na_latent(u, bias, na_q_norm[l], na_k_norm[l], n_heads=na_heads, **dims)
        qkv = _gdn_conv(u, gdn_conv[l], gw=gw, **dims)
        o_fwd, o_bwd = _gdn_scan(qkv, graw, a_row, dt_row, gw=gw, **dims)
        gdn = _gdn_out(o_fwd, o_bwd, u, gdn_norm[l], gw=gw)
        dp = _diff_prep(u, w_qk, cos_t, sin_t, n_batch=n_batch, seq=seq, gw=gw)
        df_lat = _diff_attention(dp, lam_row, w_sub, n_heads=g_heads, latent=True, **dims)

        if last:
            mix = jnp.concatenate([na_lat, gdn[:m_lat], df_lat], axis=1)
            xs = xs[:m_lat]
        else:
            na_ctx = _na_context(u, na_q_norm[l], na_k_norm[l], n_heads=na_heads, **dims)
            df_ctx = _diff_attention(dp, lam_row, w_sub, n_heads=g_heads, latent=False, **dims)
            mix = jnp.concatenate([jnp.concatenate([na_lat, na_ctx], axis=0), gdn,
                                   jnp.concatenate([df_lat, df_ctx], axis=0)], axis=1)

        xs = _matmul(mix, w_out[l].astype(BF16), tm=tm, tn=_pick_tile(d, 1024), tk=d, out_dtype=F32,
                     epilogue="gated_res", res=xs, mod=mod, gate_row=2, seq=seq, n_batch=n_batch, name="w_out")
        h2 = _modulate(xs, norm2[l], mod, 3, seq=seq, n_batch=n_batch)
        ff = _matmul(h2, w_ff1[l].astype(BF16), tm=tm, tn=_pick_tile(d_ff, 1024), tk=d, out_dtype=BF16,
                     epilogue="relu2", name="w_ff1")
        xs = _matmul(ff, w_ff2[l].astype(BF16), tm=tm, tn=_pick_tile(d, 2048), tk=min(d_ff, 2048), out_dtype=F32,
                     epilogue="gated_res", res=xs, mod=mod, gate_row=5, seq=seq, n_batch=n_batch, name="w_ff2")

    return xs[:m_lat].reshape(n_batch, seq, d)


@jax.jit
def kernel(x, c, ctx, c_ctx, ada_down, ada_up, ada_bias, norm1, w_in, na_q_norm, na_k_norm, na_rpb,
           gdn_conv, gdn_a_log, gdn_dt_bias, gdn_norm, diff_q_norm, diff_k_norm, diff_lambda, diff_subln,
           w_out, norm2, w_ff1, w_ff2):
    return _kernel_impl(x, c, ctx, c_ctx, ada_down, ada_up, ada_bias, norm1, w_in, na_q_norm, na_k_norm, na_rpb,
                        gdn_conv, gdn_a_log, gdn_dt_bias, gdn_norm, diff_q_norm, diff_k_norm, diff_lambda,
                        diff_subln, w_out, norm2, w_ff1, w_ff2)
```

```python
import functools
import math

import jax
import jax.numpy as jnp
from jax import lax
from jax.experimental import pallas as pl
from jax.experimental.pallas import tpu as pltpu

GRID_W = 64
EPS = 1e-6
N_MOD = 6
HEAD = 128
NA_WIN_H = 8
NA_WIN_W = 16
GDN_CHUNK = 64
SHORT_CONV = 5
DIFF_DIM = 64
ROPE_BASE = 10000.0
NEG = -1e30

LANES = 128
ROW_BLOCK = 256
NA_ROWS = 8
W_CAST_TILE = 1024
W_CAST_TILE_RES = 512
DIFF_Q_ROWS = 512
DIFF_SUB_ROWS = 128
VMEM_LIMIT = 56 * 1024 * 1024

F32 = jnp.float32
BF16 = jnp.bfloat16


def _params(sem, vmem=VMEM_LIMIT):
    return pltpu.CompilerParams(dimension_semantics=sem, vmem_limit_bytes=vmem)


def _pick_tile(n, target):
    best = None
    for t in range(LANES, min(n, target) + 1, LANES):
        if n % t == 0:
            best = t
    assert best is not None, (n, target)
    return best


def _dot_t(a, b):
    return lax.dot_general(a, b, (((1,), (1,)), ((), ())), preferred_element_type=F32)


def _rms(x, w):
    return x * lax.rsqrt(jnp.mean(x * x, axis=-1, keepdims=True) + EPS) * w


def _modulate_kernel(x_ref, w_ref, mod_ref, o_ref, *, shift_row):
    x = x_ref[...]
    y = _rms(x, w_ref[...])
    shift = mod_ref[0, shift_row:shift_row + 1, :]
    scale = mod_ref[0, shift_row + 1:shift_row + 2, :]
    o_ref[...] = (y * (1.0 + scale) + shift).astype(o_ref.dtype)


def _modulate(x, w, mod, shift_row, *, seq, n_batch):
    m, d = x.shape
    tr = ROW_BLOCK
    bps = seq // tr
    seg = lambda i: jnp.minimum(i // bps, n_batch)
    return pl.pallas_call(
        functools.partial(_modulate_kernel, shift_row=shift_row),
        out_shape=jax.ShapeDtypeStruct((m, d), BF16),
        grid=(m // tr,),
        in_specs=[pl.BlockSpec((tr, d), lambda i: (i, 0)),
                  pl.BlockSpec((1, d), lambda i: (0, 0)),
                  pl.BlockSpec((1, N_MOD, d), lambda i: (seg(i), 0, 0))],
        out_specs=pl.BlockSpec((tr, d), lambda i: (i, 0)),
        compiler_params=_params(("parallel",)),
        name="modulate",
    )(x, w.reshape(1, d), mod)


def _mm_kernel(*refs, nk, epilogue, gate_row, cast_w):
    if epilogue == "gated_res":
        a_ref, w_ref, res_ref, mod_ref, o_ref = refs[:5]
        scratch = list(refs[5:])
    else:
        a_ref, w_ref, o_ref = refs[:3]
        scratch = list(refs[3:])

    def finish(acc):
        if epilogue == "relu2":
            r = jnp.maximum(acc, 0.0)
            acc = r * r
        elif epilogue == "gated_res":
            acc = res_ref[...] + mod_ref[0, gate_row:gate_row + 1, :] * acc
        o_ref[...] = acc.astype(o_ref.dtype)

    if cast_w:
        w16_ref = scratch.pop(0)

        @pl.when(pl.program_id(1) == 0)
        def _():
            w16_ref[...] = w_ref[...].astype(BF16)

        w = w16_ref[...]
    else:
        w = w_ref[...]

    if nk == 1:
        finish(jnp.dot(a_ref[...], w, preferred_element_type=F32))
    else:
        acc_ref = scratch[0]
        k = pl.program_id(2)

        @pl.when(k == 0)
        def _():
            acc_ref[...] = jnp.zeros_like(acc_ref)

        acc_ref[...] += jnp.dot(a_ref[...], w, preferred_element_type=F32)

        @pl.when(k == nk - 1)
        def _():
            finish(acc_ref[...])


def _matmul(a, w, *, tm, tn, tk, out_dtype, epilogue="none", res=None, mod=None, gate_row=0,
            seq=None, n_batch=None, layer=None, n_cols=None, rows_outer=False, name="matmul"):
    m, kdim = a.shape
    cast_w = w.ndim == 3
    n = n_cols if cast_w else w.shape[1]
    assert m % tm == 0 and n % tn == 0 and kdim % tk == 0, (m, n, kdim, tm, tn, tk)
    nk = kdim // tk
    if rows_outer:
        assert not cast_w
        ij = lambda f: (lambda i, j, k: f(i, j, k))
        grid = (m // tm, n // tn, nk)
    else:
        ij = lambda f: (lambda j, i, k: f(i, j, k))
        grid = (n // tn, m // tm, nk)
    if cast_w:
        assert nk == 1
        w_spec = pl.BlockSpec((None, tk, tn), ij(lambda i, j, k: (layer, k, j)))
    else:
        w_spec = pl.BlockSpec((tk, tn), ij(lambda i, j, k: (k, j)))
    in_specs = [pl.BlockSpec((tm, tk), ij(lambda i, j, k: (i, k))), w_spec]
    args = [a, w]
    if epilogue == "gated_res":
        bps = seq // tm
        in_specs += [pl.BlockSpec((tm, tn), ij(lambda i, j, k: (i, j))),
                     pl.BlockSpec((1, N_MOD, tn), ij(lambda i, j, k: (jnp.minimum(i // bps, n_batch), 0, j)))]
        args += [res, mod]
    scratch = [pltpu.VMEM((tk, tn), BF16)] if cast_w else []
    scratch += [pltpu.VMEM((tm, tn), F32)] if nk > 1 else []
    return pl.pallas_call(
        functools.partial(_mm_kernel, nk=nk, epilogue=epilogue, gate_row=gate_row, cast_w=cast_w),
        out_shape=jax.ShapeDtypeStruct((m, n), out_dtype),
        grid=grid,
        in_specs=in_specs,
        out_specs=pl.BlockSpec((tm, tn), ij(lambda i, j, k: (i, j))),
        scratch_shapes=scratch,
        compiler_params=_params(("parallel", "arbitrary", "arbitrary")),
        name=name,
    )(*args)


def _na_kernel(q_ref, kp_ref, kc_ref, kn_ref, vp_ref, vc_ref, vn_ref, kx_ref, vx_ref,
               bias_ref, qw_ref, kw_ref, o_ref):
    scale = HEAD ** -0.5
    qw = qw_ref[...]
    kw = kw_ref[...]
    q = (_rms(q_ref[...], qw) * scale).astype(BF16)
    kl = jnp.concatenate([_rms(kp_ref[...], kw), _rms(kc_ref[...], kw), _rms(kn_ref[...], kw)],
                         axis=0).astype(BF16)
    kx = _rms(kx_ref[...], kw).astype(BF16)
    s_loc = _dot_t(q, kl) + bias_ref[0, 0]
    s_ctx = _dot_t(q, kx)
    mx = jnp.maximum(jnp.max(s_loc, axis=-1, keepdims=True), jnp.max(s_ctx, axis=-1, keepdims=True))
    e_loc = jnp.exp(s_loc - mx)
    e_ctx = jnp.exp(s_ctx - mx)
    denom = jnp.sum(e_loc, axis=-1, keepdims=True) + jnp.sum(e_ctx, axis=-1, keepdims=True)
    vl = jnp.concatenate([vp_ref[...], vc_ref[...], vn_ref[...]], axis=0).astype(BF16)
    o = jnp.dot(e_loc.astype(BF16), vl, preferred_element_type=F32)
    o = o + jnp.dot(e_ctx.astype(BF16), vx_ref[...].astype(BF16), preferred_element_type=F32)
    o_ref[...] = (o / denom).astype(o_ref.dtype)


def _na_bias_tables(rpb):
    h = rpb.shape[0]
    cq = jnp.arange(GRID_W)
    dc = jnp.clip(cq[None, :] - cq[:, None], -(NA_WIN_W - 1), NA_WIN_W - 1) + (NA_WIN_W - 1)
    col_start = jnp.clip(cq - NA_WIN_W // 2, 0, GRID_W - NA_WIN_W)
    col_ok = (cq[None, :] >= col_start[:, None]) & (cq[None, :] < col_start[:, None] + NA_WIN_W)
    t = jnp.where(col_ok, rpb[:, :, dc], NEG)
    band = jnp.stack([t[:, d0:d0 + NA_WIN_H] for d0 in range(NA_WIN_H)], axis=1)
    band = band.transpose(0, 1, 3, 2, 4).reshape(h, NA_WIN_H, GRID_W, NA_WIN_H * GRID_W)
    half = NA_WIN_H // 2
    n_keys = 3 * NA_ROWS * GRID_W
    tabs = []
    for variant in range(3):
        slabs = []
        for rq in range(NA_ROWS):
            start = NA_ROWS + (max(rq - half, 0), rq - half, min(rq - half, 0))[variant]
            left = start * GRID_W
            slabs.append(jnp.pad(band[:, start - rq - 1], ((0, 0), (0, 0), (left, n_keys - left - NA_WIN_H * GRID_W)),
                                 constant_values=NEG))
        tabs.append(jnp.stack(slabs, axis=1).reshape(h, NA_ROWS * GRID_W, n_keys))
    return jnp.stack(tabs)


def _na_latent(u, bias, qw, kw, *, n_batch, seq, ctx_len, n_heads):
    tq = NA_ROWS * GRID_W
    nj = seq // tq
    assert nj >= 2 and seq % tq == 0
    kcol = n_heads
    vcol = 2 * n_heads
    ctx_blk = n_batch * seq // ctx_len
    qmap = lambda b, h, j: (b * nj + j, h)
    prev = lambda col: (lambda b, h, j: (b * nj + jnp.maximum(j - 1, 0), col + h))
    cur = lambda col: (lambda b, h, j: (b * nj + j, col + h))
    nxt = lambda col: (lambda b, h, j: (b * nj + jnp.minimum(j + 1, nj - 1), col + h))
    cx = lambda col: (lambda b, h, j: (ctx_blk + b, col + h))
    variant = lambda j: jnp.where(j == 0, 0, jnp.where(j == nj - 1, 2, 1))
    blk = lambda rows, imap: pl.BlockSpec((rows, HEAD), imap)
    return pl.pallas_call(
        _na_kernel,
        out_shape=jax.ShapeDtypeStruct((n_batch * seq, n_heads * HEAD), BF16),
        grid=(n_batch, n_heads, nj),
        in_specs=[blk(tq, qmap),
                  blk(tq, prev(kcol)), blk(tq, cur(kcol)), blk(tq, nxt(kcol)),
                  blk(tq, prev(vcol)), blk(tq, cur(vcol)), blk(tq, nxt(vcol)),
                  blk(ctx_len, cx(kcol)), blk(ctx_len, cx(vcol)),
                  pl.BlockSpec((1, 1, tq, 3 * tq), lambda b, h, j: (variant(j), h, 0, 0)),
                  pl.BlockSpec((1, HEAD), lambda b, h, j: (0, 0)),
                  pl.BlockSpec((1, HEAD), lambda b, h, j: (0, 0))],
        out_specs=blk(tq, qmap),
        compiler_params=_params(("parallel", "parallel", "arbitrary")),
        name="na_latent",
    )(u, u, u, u, u, u, u, u, u, bias, qw.reshape(1, HEAD), kw.reshape(1, HEAD))


def _na_ctx_kernel(q_ref, k_ref, v_ref, qw_ref, kw_ref, o_ref):
    q = (_rms(q_ref[...], qw_ref[...]) * (HEAD ** -0.5)).astype(BF16)
    k = _rms(k_ref[...], kw_ref[...]).astype(BF16)
    s = _dot_t(q, k)
    e = jnp.exp(s - jnp.max(s, axis=-1, keepdims=True))
    o = jnp.dot(e.astype(BF16), v_ref[...].astype(BF16), preferred_element_type=F32)
    o_ref[...] = (o / jnp.sum(e, axis=-1, keepdims=True)).astype(o_ref.dtype)


def _na_context(u, qw, kw, *, n_batch, seq, ctx_len, n_heads):
    ctx_blk = n_batch * seq // ctx_len
    cx = lambda col: (lambda b, h: (ctx_blk + b, col + h))
    blk = lambda imap: pl.BlockSpec((ctx_len, HEAD), imap)
    return pl.pallas_call(
        _na_ctx_kernel,
        out_shape=jax.ShapeDtypeStruct((n_batch * ctx_len, n_heads * HEAD), BF16),
        grid=(n_batch, n_heads),
        in_specs=[blk(cx(0)), blk(cx(n_heads)), blk(cx(2 * n_heads)),
                  pl.BlockSpec((1, HEAD), lambda b, h: (0, 0)),
                  pl.BlockSpec((1, HEAD), lambda b, h: (0, 0))],
        out_specs=blk(lambda b, h: (b, h)),
        compiler_params=_params(("parallel", "parallel")),
        name="na_context",
    )(u, u, u, qw.reshape(1, HEAD), kw.reshape(1, HEAD))


def _gdn_conv_kernel(prev_ref, cur_ref, next_ref, w_ref, o_ref, *, lat_blocks, seq_blocks, ctx_blocks, n_heads):
    i = pl.program_id(0)
    j = pl.program_id(1)
    pos = jnp.where(i < lat_blocks, i % seq_blocks, (i - lat_blocks) % ctx_blocks)
    length = jnp.where(i < lat_blocks, seq_blocks, ctx_blocks)
    first = (pos == 0).astype(F32)
    last = (pos == length - 1).astype(F32)
    tr = cur_ref.shape[0]
    n_ext = tr + 16
    is_qk = j < 2
    scale = jnp.where(j == 0, HEAD ** -0.5, 1.0)
    for h in range(n_heads):
        sl = slice(h * HEAD, (h + 1) * HEAD)
        ext = jnp.concatenate([prev_ref[:, sl] * (1.0 - first), cur_ref[:, sl], next_ref[:, sl] * (1.0 - last)],
                              axis=0)
        acc = jnp.zeros((tr, HEAD), F32)
        for tap in range(SHORT_CONV):
            d = tap - SHORT_CONV // 2
            shifted = ext if d == 0 else pltpu.roll(ext, (-d) % n_ext, axis=0)
            acc = acc + shifted[8:8 + tr] * w_ref[tap:tap + 1, sl]
        y = acc * (1.0 / (1.0 + jnp.exp(-acc)))
        norm = lax.rsqrt(jnp.sum(y * y, axis=-1, keepdims=True) + EPS) * scale
        o_ref[:, sl] = y * jnp.where(is_qk, norm, 1.0)


def _gdn_conv(u, conv_w, *, n_batch, seq, ctx_len, gw):
    m = u.shape[0]
    tr = ROW_BLOCK
    n_heads = gw // HEAD
    lat_blocks = n_batch * seq // tr
    halo = tr // 8
    nblk8 = m // 8
    return pl.pallas_call(
        functools.partial(_gdn_conv_kernel, lat_blocks=lat_blocks, seq_blocks=seq // tr,
                          ctx_blocks=ctx_len // tr, n_heads=n_heads),
        out_shape=jax.ShapeDtypeStruct((m, 3 * gw), F32),
        grid=(m // tr, 3),
        in_specs=[pl.BlockSpec((8, gw), lambda i, j: (jnp.maximum(i * halo - 1, 0), 2 + j)),
                  pl.BlockSpec((tr, gw), lambda i, j: (i, 2 + j)),
                  pl.BlockSpec((8, gw), lambda i, j: (jnp.minimum((i + 1) * halo, nblk8 - 1), 2 + j)),
                  pl.BlockSpec((SHORT_CONV, gw), lambda i, j: (0, j))],
        out_specs=pl.BlockSpec((tr, gw), lambda i, j: (i, j)),
        compiler_params=_params(("parallel", "parallel")),
        name="gdn_conv",
    )(u, u, u, conv_w)


def _cumsum_rows(x, reverse):
    n = x.shape[0]
    row = lax.broadcasted_iota(jnp.int32, x.shape, 0)
    s = 1
    while s < n:
        if reverse:
            x = x + jnp.where(row < n - s, pltpu.roll(x, n - s, axis=0), 0.0)
        else:
            x = x + jnp.where(row >= s, pltpu.roll(x, s, axis=0), 0.0)
        s *= 2
    return x


def _unit_triangular_inverse(lmats, ij_xor):
    c = lmats[0].shape[0]
    eye = (ij_xor == 0).astype(F32)
    dot = functools.partial(jnp.dot, preferred_element_type=F32)
    xs = [eye - jnp.where(ij_xor < 2, lm, 0.0) for lm in lmats]
    blk = 2
    while blk < c:
        pick = (ij_xor // blk) == 1
        x16 = [x.astype(BF16) for x in xs]
        xc = [dot(x, jnp.where(pick, lm, 0.0).astype(BF16)) for x, lm in zip(x16, lmats)]
        xcx = [dot(y.astype(BF16), x) for y, x in zip(xc, x16)]
        xs = [x - z for x, z in zip(xs, xcx)]
        blk *= 2
    return xs


def _gdn_scan_kernel(qf_ref, kf_ref, vf_ref, gf_ref, qb_ref, kb_ref, vb_ref, gb_ref, a_ref, dt_ref,
                     of_ref, ob_ref, state_ref, *, n_heads):
    @pl.when(pl.program_id(1) == 0)
    def _():
        state_ref[...] = jnp.zeros_like(state_ref)

    c = GDN_CHUNK
    dot = functools.partial(jnp.dot, preferred_element_type=F32)
    ii = lax.broadcasted_iota(jnp.int32, (c, c), 0)
    jj = lax.broadcasted_iota(jnp.int32, (c, c), 1)
    ij_xor = ii ^ jj

    qs, ks, vs, betas, gcols, grows, egs, kes, cds, incls, stricts, outs = ([] for _ in range(12))
    for d, (q_ref, k_ref, v_ref, g_ref, o_ref) in enumerate(
            ((qf_ref, kf_ref, vf_ref, gf_ref, of_ref), (qb_ref, kb_ref, vb_ref, gb_ref, ob_ref))):
        reverse = d == 1
        raw = g_ref[...]
        beta_all = 1.0 / (1.0 + jnp.exp(-raw))
        z = raw + dt_ref[...]
        softplus = jnp.maximum(z, 0.0) + jnp.log1p(jnp.exp(-jnp.abs(z)))
        gcum = _cumsum_rows(a_ref[...] * softplus, reverse)
        gcum_t = gcum.T
        last_row = 0 if reverse else c - 1
        g_last = gcum[last_row:last_row + 1, :]
        eg_all = jnp.exp(gcum)
        ke_all = jnp.exp(g_last - gcum)
        cd_all = jnp.exp(g_last)
        incl = (jj >= ii) if reverse else (jj <= ii)
        strict = (jj > ii) if reverse else (jj < ii)
        for h in range(n_heads):
            bcol = d * n_heads + h
            gcol = 2 * n_heads + bcol
            sl = slice(h * HEAD, (h + 1) * HEAD)
            qs.append(q_ref[:, sl])
            ks.append(k_ref[:, sl])
            vs.append(v_ref[:, sl])
            betas.append(beta_all[:, bcol:bcol + 1])
            gcols.append(gcum[:, gcol:gcol + 1])
            grows.append(gcum_t[gcol:gcol + 1, :])
            egs.append(eg_all[:, gcol:gcol + 1])
            kes.append(ke_all[:, gcol:gcol + 1])
            cds.append(cd_all[:, gcol:gcol + 1])
            incls.append(incl)
            stricts.append(strict)
            outs.append((o_ref, sl))
    n = len(qs)
    rng = range(n)

    decay = [jnp.exp(jnp.where(incls[i], gcols[i] - grows[i], NEG)) for i in rng]
    kbs = [ks[i] * betas[i] for i in rng]
    kk = [_dot_t(jnp.concatenate([kbs[i], qs[i]], axis=0).astype(BF16), ks[i].astype(BF16)) for i in rng]
    lmats = [jnp.where(stricts[i], kk[i][:c] * decay[i], 0.0) for i in rng]
    amats = [(kk[i][c:] * decay[i]).astype(BF16) for i in rng]
    tinvs = _unit_triangular_inverse(lmats, ij_xor)
    rhs = [jnp.concatenate([vs[i] * betas[i], kbs[i] * egs[i]], axis=1).astype(BF16) for i in rng]
    uw = [dot(tinvs[i].astype(BF16), rhs[i]).astype(BF16) for i in rng]
    auw = [dot(amats[i], uw[i]) for i in rng]
    kuw = [lax.dot_general((ks[i] * kes[i]).astype(BF16), uw[i], (((0,), (0,)), ((), ())),
                           preferred_element_type=F32) for i in rng]
    q2 = [(qs[i] * egs[i] - auw[i][:, HEAD:]).astype(BF16) for i in rng]
    states = [state_ref[i] for i in rng]
    s16 = [st.astype(BF16) for st in states]
    out = [auw[i][:, :HEAD] + dot(q2[i], s16[i]) for i in rng]
    new_states = [states[i] * cds[i] + kuw[i][:, :HEAD] - dot(kuw[i][:, HEAD:].astype(BF16), s16[i]) for i in rng]
    for i in rng:
        o_ref, sl = outs[i]
        o_ref[:, sl] = out[i]
        state_ref[i] = new_states[i]


def _gdn_scan(qkv, graw, a_row, dt_row, *, n_batch, seq, ctx_len, gw):
    m = qkv.shape[0]
    c = GDN_CHUNK
    n_heads = gw // HEAD
    n_ctx = ctx_len // c
    n_lat = seq // c

    def row_block(b, t, d):
        tl = t - n_ctx
        c_ctx = t if d == 0 else n_ctx - 1 - t
        c_lat = tl if d == 0 else n_lat - 1 - tl
        return jnp.where(t < n_ctx, (n_batch * seq + b * ctx_len) // c + c_ctx, (b * seq) // c + c_lat)

    def spec(col, d):
        return pl.BlockSpec((c, gw), lambda b, t: (row_block(b, t, d), col))

    def gspec(d):
        return pl.BlockSpec((c, LANES), lambda b, t: (row_block(b, t, d), 0))

    const = pl.BlockSpec((1, LANES), lambda b, t: (0, 0))
    in_specs = []
    for d in range(2):
        in_specs += [spec(0, d), spec(1, d), spec(2, d), gspec(d)]
    in_specs += [const, const]
    args = [qkv, qkv, qkv, graw, qkv, qkv, qkv, graw, a_row, dt_row]
    return pl.pallas_call(
        functools.partial(_gdn_scan_kernel, n_heads=n_heads),
        out_shape=[jax.ShapeDtypeStruct((m, gw), F32), jax.ShapeDtypeStruct((m, gw), F32)],
        grid=(n_batch, n_ctx + n_lat),
        in_specs=in_specs,
        out_specs=[spec(0, 0), spec(0, 1)],
        scratch_shapes=[pltpu.VMEM((2 * n_heads, HEAD, HEAD), F32)],
        compiler_params=_params(("parallel", "arbitrary")),
        name="gdn_scan",
    )(*args)


def _gdn_out_kernel(of_ref, ob_ref, z_ref, w_ref, o_ref, *, n_heads):
    w = w_ref[...]
    for h in range(n_heads):
        sl = slice(h * HEAD, (h + 1) * HEAD)
        o = _rms(of_ref[:, sl] + ob_ref[:, sl], w)
        z = z_ref[:, sl]
        o_ref[:, sl] = (o * (z / (1.0 + jnp.exp(-z)))).astype(o_ref.dtype)


def _gdn_out(o_fwd, o_bwd, u, w, *, gw):
    m = o_fwd.shape[0]
    tr = ROW_BLOCK
    blk = lambda col: pl.BlockSpec((tr, gw), lambda i: (i, col))
    return pl.pallas_call(
        functools.partial(_gdn_out_kernel, n_heads=gw // HEAD),
        out_shape=jax.ShapeDtypeStruct((m, gw), BF16),
        grid=(m // tr,),
        in_specs=[blk(0), blk(0), blk(5), pl.BlockSpec((1, HEAD), lambda i: (0, 0))],
        out_specs=blk(0),
        compiler_params=_params(("parallel",)),
        name="gdn_out",
    )(o_fwd, o_bwd, u, w.reshape(1, HEAD))


def _diff_prep_kernel(x_ref, w_ref, cos_ref, sin_ref, o_ref, *, n_heads):
    j = pl.program_id(1)

    @pl.when(j < 2)
    def _():
        w = w_ref[0]
        cos = cos_ref[...]
        sin = sin_ref[...]
        lane = lax.broadcasted_iota(jnp.int32, (x_ref.shape[0], HEAD), 1)
        low = lane < DIFF_DIM
        first_half = (lane % DIFF_DIM) < DIFF_DIM // 2
        scale = jnp.where(j == 0, DIFF_DIM ** -0.5 * math.log2(math.e), 1.0)
        for h in range(n_heads):
            sl = slice(h * HEAD, (h + 1) * HEAD)
            x = x_ref[:, sl]
            sq = x * x
            s_low = jnp.sum(jnp.where(low, sq, 0.0), axis=-1, keepdims=True)
            s_high = jnp.sum(jnp.where(low, 0.0, sq), axis=-1, keepdims=True)
            ms = jnp.where(low, s_low, s_high) * (1.0 / DIFF_DIM)
            y = x * lax.rsqrt(ms + EPS) * w
            swapped = jnp.where(first_half, pltpu.roll(y, HEAD - DIFF_DIM // 2, axis=1),
                                pltpu.roll(y, DIFF_DIM // 2, axis=1))
            o_ref[:, sl] = ((y * cos + swapped * sin) * scale).astype(o_ref.dtype)

    @pl.when(j == 2)
    def _():
        o_ref[...] = x_ref[...].astype(o_ref.dtype)


def _diff_prep(u, w_qk, cos_t, sin_t, *, n_batch, seq, gw):
    m = u.shape[0]
    tr = ROW_BLOCK
    lat_blocks = n_batch * seq // tr
    seq_blocks = seq // tr
    rope_map = lambda i, j: (jnp.where(i < lat_blocks, i % seq_blocks, seq_blocks), 0)
    return pl.pallas_call(
        functools.partial(_diff_prep_kernel, n_heads=gw // HEAD),
        out_shape=jax.ShapeDtypeStruct((m, 3 * gw), BF16),
        grid=(m // tr, 3),
        in_specs=[pl.BlockSpec((tr, gw), lambda i, j: (i, j)),
                  pl.BlockSpec((1, 1, HEAD), lambda i, j: (jnp.minimum(j, 1), 0, 0)),
                  pl.BlockSpec((tr, HEAD), rope_map),
                  pl.BlockSpec((tr, HEAD), rope_map)],
        out_specs=pl.BlockSpec((tr, gw), lambda i, j: (i, j)),
        compiler_params=_params(("parallel", "parallel")),
        name="diff_prep",
    )(u, w_qk, cos_t, sin_t)


def _diff_attn_kernel(*refs, n_seg):
    q_ref = refs[0]
    k_refs = refs[1:1 + n_seg]
    v_refs = refs[1 + n_seg:1 + 2 * n_seg]
    lam_ref, w_ref, o_ref = refs[1 + 2 * n_seg:]
    tq = q_ref.shape[0]
    ts = min(tq, DIFF_SUB_ROWS)
    subs = range(tq // ts)
    lane = lax.broadcasted_iota(jnp.int32, (ts, HEAD), 1)
    zero = jnp.zeros((ts, HEAD), q_ref.dtype)
    v1s = [jnp.concatenate([v_ref[...], jnp.ones(v_ref.shape, v_ref.dtype)], axis=1) for v_ref in v_refs]
    q2s = []
    for i in subs:
        q = q_ref[i * ts:(i + 1) * ts, :]
        q2s.append(jnp.concatenate([jnp.where(lane < DIFF_DIM, q, zero), jnp.where(lane < DIFF_DIM, zero, q)], axis=0))
    scores = [[_dot_t(q2, k_ref[...]) for k_ref in k_refs] for q2 in q2s]
    mxs = [functools.reduce(jnp.maximum, [jnp.max(s, axis=-1, keepdims=True) for s in sc]) for sc in scores]
    es = [[jnp.exp2((s - mx).astype(BF16)) for s in sc] for sc, mx in zip(scores, mxs)]
    accs = [functools.reduce(jnp.add, [jnp.dot(e, v1, preferred_element_type=F32) for e, v1 in zip(e_seg, v1s)])
            for e_seg in es]
    for i, acc in zip(subs, accs):
        r = acc[:, :HEAD] / acc[:, HEAD:HEAD + 1]
        o = r[:ts] - lam_ref[0:1, 0:1] * r[ts:]
        o_ref[i * ts:(i + 1) * ts, :] = _rms(o, w_ref[...]).astype(o_ref.dtype)


def _diff_attention(dp, lam_row, w_sub, *, n_batch, seq, ctx_len, n_heads, latent):
    kcol = n_heads
    vcol = 2 * n_heads
    ctx_blk = n_batch * seq // ctx_len
    if latent:
        tq = min(seq, DIFF_Q_ROWS)
        nq = seq // tq
        qmap = lambda b, h, i: (b * nq + i, h)
        segs = [(seq, lambda col: (lambda b, h, i: (b, col + h))),
                (ctx_len, lambda col: (lambda b, h, i: (ctx_blk + b, col + h)))]
        out_rows = n_batch * seq
        omap = qmap
    else:
        tq = ctx_len
        nq = 1
        qmap = lambda b, h, i: (ctx_blk + b, h)
        segs = [(ctx_len, lambda col: (lambda b, h, i: (ctx_blk + b, col + h)))]
        out_rows = n_batch * ctx_len
        omap = lambda b, h, i: (b, h)
    in_specs = [pl.BlockSpec((tq, HEAD), qmap)]
    in_specs += [pl.BlockSpec((rows, HEAD), mk(kcol)) for rows, mk in segs]
    in_specs += [pl.BlockSpec((rows, HEAD), mk(vcol)) for rows, mk in segs]
    in_specs += [pl.BlockSpec((1, HEAD), lambda b, h, i: (0, 0))] * 2
    n_seg = len(segs)
    return pl.pallas_call(
        functools.partial(_diff_attn_kernel, n_seg=n_seg),
        out_shape=jax.ShapeDtypeStruct((out_rows, n_heads * HEAD), BF16),
        grid=(n_batch, n_heads, nq),
        in_specs=in_specs,
        out_specs=pl.BlockSpec((tq, HEAD), omap),
        compiler_params=_params(("parallel", "parallel", "arbitrary")),
        name="diff_attn_latent" if latent else "diff_attn_context",
    )(*([dp] * (1 + 2 * n_seg)), lam_row, w_sub)


def _rope_tables(seq, extra_rows):
    n_freq = DIFF_DIM // 4
    inv = ROPE_BASE ** (-jnp.arange(n_freq, dtype=F32) / n_freq)
    t = jnp.arange(seq)
    row = (t // GRID_W).astype(F32)
    col = (t % GRID_W).astype(F32)
    ang = jnp.concatenate([row[:, None] * inv, col[:, None] * inv], axis=-1)
    cos, sin = jnp.cos(ang), jnp.sin(ang)
    cos_t = jnp.tile(cos, (1, HEAD // cos.shape[1]))
    sin_t = jnp.tile(jnp.concatenate([-sin, sin], axis=-1), (1, HEAD // (2 * sin.shape[1])))
    cos_t = jnp.concatenate([cos_t, jnp.ones((extra_rows, HEAD), F32)], axis=0)
    sin_t = jnp.concatenate([sin_t, jnp.zeros((extra_rows, HEAD), F32)], axis=0)
    return cos_t, sin_t


def _kernel_impl(x, c, ctx, c_ctx, ada_down, ada_up, ada_bias, norm1, w_in, na_q_norm, na_k_norm, na_rpb,
                 gdn_conv, gdn_a_log, gdn_dt_bias, gdn_norm, diff_q_norm, diff_k_norm, diff_lambda, diff_subln,
                 w_out, norm2, w_ff1, w_ff2):
    n_batch, seq, d = x.shape
    ctx_len = ctx.shape[1]
    depth = w_in.shape[0]
    d_ff = w_ff1.shape[2]
    naw = d // 4
    gw = 3 * d // 8
    na_heads = naw // HEAD
    g_heads = gw // HEAD
    n_gate = 4 * g_heads
    m_lat = n_batch * seq
    m_all = m_lat + n_batch * ctx_len
    n_main = 9 * gw
    gate_off = 3 * naw + 4 * gw
    assert w_in.shape[2] == n_main + n_gate and n_gate <= LANES
    assert seq % (NA_ROWS * GRID_W) == 0 and ctx_len % ROW_BLOCK == 0 and seq % ctx_len == 0

    xs = jnp.concatenate([x.reshape(m_lat, d), ctx.reshape(n_batch * ctx_len, d)], axis=0)
    cond = jnp.concatenate([c, c_ctx[None, :]], axis=0)
    cond = cond * (1.0 / (1.0 + jnp.exp(-cond)))
    cond = jnp.pad(cond, ((0, 16 - (n_batch + 1)), (0, 0))).astype(BF16)
    cos_t, sin_t = _rope_tables(seq, ROW_BLOCK)
    dims = dict(n_batch=n_batch, seq=seq, ctx_len=ctx_len)
    tm = 512

    for l in range(depth):
        last = l == depth - 1
        lam_init = 0.8 - 0.6 * math.exp(-0.3 * l)
        w_diff = w_in[l][:, gate_off + n_gate:].astype(BF16)
        w_gate = jnp.pad(w_in[l][:, gate_off:gate_off + n_gate], ((0, 0), (0, LANES - n_gate))).astype(BF16)
        a_row = jnp.pad(-jnp.exp(gdn_a_log[l]).reshape(1, -1), ((0, 0), (2 * g_heads, LANES - n_gate)))
        dt_row = jnp.pad(gdn_dt_bias[l].reshape(1, -1), ((0, 0), (2 * g_heads, LANES - n_gate)))
        lv = diff_lambda[l]
        lam = jnp.exp(jnp.sum(lv[0] * lv[1])) - jnp.exp(jnp.sum(lv[2] * lv[3])) + lam_init
        lam_row = jnp.full((1, HEAD), lam, F32)
        w_sub = (diff_subln[l] * (1.0 - lam_init)).reshape(1, HEAD)
        w_qk = jnp.stack([jnp.tile(diff_q_norm[l], 2), jnp.tile(diff_k_norm[l], 2)]).reshape(2, 1, HEAD)
        bias = _na_bias_tables(na_rpb[l])

        low = _matmul(cond, ada_down[l].astype(BF16), tm=16, tn=ada_down.shape[2], tk=d, out_dtype=BF16,
                      name="ada_down")
        mod = _matmul(low, ada_up[l].astype(BF16), tm=16, tn=_pick_tile(N_MOD * d, 4096), tk=ada_down.shape[2],
                      out_dtype=F32, name="ada_up")
        mod = (mod[:n_batch + 1] + ada_bias[l][None, :]).reshape(n_batch + 1, N_MOD, d)

        h1 = _modulate(xs, norm1[l], mod, 0, seq=seq, n_batch=n_batch)
        u = _matmul(h1, w_in, layer=l, n_cols=gate_off, tm=tm, tn=_pick_tile(gate_off, W_CAST_TILE), tk=d,
                    out_dtype=F32, name="w_in")
        u_diff = _matmul(h1, w_diff, tm=tm, tn=_pick_tile(3 * gw, 1536), tk=d, out_dtype=F32, name="w_in_diff")
        graw = _matmul(h1, w_gate, tm=tm, tn=LANES, tk=d, out_dtype=F32, name="w_in_gates")

        na_lat = _na_latent(u, bias, na_q_norm[l], na_k_norm[l], n_heads=na_heads, **dims)
        qkv = _gdn_conv(u, gdn_conv[l], gw=gw, **dims)
        o_fwd, o_bwd = _gdn_scan(qkv, graw, a_row, dt_row, gw=gw, **dims)
        gdn = _gdn_out(o_fwd, o_bwd, u, gdn_norm[l], gw=gw)
        dp = _diff_prep(u_diff, w_qk, cos_t, sin_t, n_batch=n_batch, seq=seq, gw=gw)
        df_lat = _diff_attention(dp, lam_row, w_sub, n_heads=g_heads, latent=True, **dims)

        if last:
            mix = jnp.concatenate([na_lat, gdn[:m_lat], df_lat], axis=1)
            xs = xs[:m_lat]
        else:
            na_ctx = _na_context(u, na_q_norm[l], na_k_norm[l], n_heads=na_heads, **dims)
            df_ctx = _diff_attention(dp, lam_row, w_sub, n_heads=g_heads, latent=False, **dims)
            mix = jnp.concatenate([jnp.concatenate([na_lat, na_ctx], axis=0), gdn,
                                   jnp.concatenate([df_lat, df_ctx], axis=0)], axis=1)

        xs = _matmul(mix, w_out, layer=l, n_cols=d, tm=tm, tn=_pick_tile(d, W_CAST_TILE_RES), tk=d, out_dtype=F32,
                     epilogue="gated_res", res=xs, mod=mod, gate_row=2, seq=seq, n_batch=n_batch, name="w_out")
        h2 = _modulate(xs, norm2[l], mod, 3, seq=seq, n_batch=n_batch)
        ff = _matmul(h2, w_ff1, layer=l, n_cols=d_ff, tm=tm, tn=_pick_tile(d_ff, W_CAST_TILE), tk=d, out_dtype=BF16,
                     epilogue="relu2", name="w_ff1")
        xs = _matmul(ff, w_ff2[l].astype(BF16), tm=tm, tn=2 * LANES, tk=d_ff, rows_outer=True, out_dtype=F32,
                     epilogue="gated_res", res=xs, mod=mod, gate_row=5, seq=seq, n_batch=n_batch, name="w_ff2")

    return xs[:m_lat].reshape(n_batch, seq, d)


@jax.jit
def kernel(x, c, ctx, c_ctx, ada_down, ada_up, ada_bias, norm1, w_in, na_q_norm, na_k_norm, na_rpb,
           gdn_conv, gdn_a_log, gdn_dt_bias, gdn_norm, diff_q_norm, diff_k_norm, diff_lambda, diff_subln,
           w_out, norm2, w_ff1, w_ff2):
    return _kernel_impl(x, c, ctx, c_ctx, ada_down, ada_up, ada_bias, norm1, w_in, na_q_norm, na_k_norm, na_rpb,
                        gdn_conv, gdn_a_log, gdn_dt_bias, gdn_norm, diff_q_norm, diff_k_norm, diff_lambda,
                        diff_subln, w_out, norm2, w_ff1, w_ff2)
```

```python
import functools
import math

import jax
import jax.numpy as jnp
from jax import lax
from jax.experimental import pallas as pl
from jax.experimental.pallas import tpu as pltpu

GRID_W = 64
EPS = 1e-6
N_MOD = 6
HEAD = 128
NA_WIN_H = 8
NA_WIN_W = 16
GDN_CHUNK = 64
SHORT_CONV = 5
DIFF_DIM = 64
ROPE_BASE = 10000.0
NEG = -1e30

LANES = 128
ROW_BLOCK = 256
NA_ROWS = 8
NA_SUB_ROWS = 2
W_CAST_TILE = 1024
DIFF_Q_ROWS = 512
DIFF_SUB_ROWS = 128
VMEM_LIMIT = 56 * 1024 * 1024

F32 = jnp.float32
BF16 = jnp.bfloat16


def _params(sem, vmem=VMEM_LIMIT):
    return pltpu.CompilerParams(dimension_semantics=sem, vmem_limit_bytes=vmem)


def _pick_tile(n, target):
    best = None
    for t in range(LANES, min(n, target) + 1, LANES):
        if n % t == 0:
            best = t
    assert best is not None, (n, target)
    return best


def _dot_t(a, b):
    return lax.dot_general(a, b, (((1,), (1,)), ((), ())), preferred_element_type=F32)


def _rms(x, w):
    return x * lax.rsqrt(jnp.mean(x * x, axis=-1, keepdims=True) + EPS) * w


def _modulate_kernel(x_ref, w_ref, mod_ref, o_ref, *, shift_row):
    x = x_ref[...]
    y = _rms(x, w_ref[...])
    shift = mod_ref[0, shift_row:shift_row + 1, :]
    scale = mod_ref[0, shift_row + 1:shift_row + 2, :]
    o_ref[...] = (y * (1.0 + scale) + shift).astype(o_ref.dtype)


def _modulate(x, w, mod, shift_row, *, seq, n_batch):
    m, d = x.shape
    tr = ROW_BLOCK
    bps = seq // tr
    seg = lambda i: jnp.minimum(i // bps, n_batch)
    return pl.pallas_call(
        functools.partial(_modulate_kernel, shift_row=shift_row),
        out_shape=jax.ShapeDtypeStruct((m, d), BF16),
        grid=(m // tr,),
        in_specs=[pl.BlockSpec((tr, d), lambda i: (i, 0)),
                  pl.BlockSpec((1, d), lambda i: (0, 0)),
                  pl.BlockSpec((1, N_MOD, d), lambda i: (seg(i), 0, 0))],
        out_specs=pl.BlockSpec((tr, d), lambda i: (i, 0)),
        compiler_params=_params(("parallel",)),
        name="modulate",
    )(x, w.reshape(1, d), mod)


def _mm_kernel(*refs, nk, epilogue, gate_row, cast_w):
    if epilogue == "gated_res":
        a_ref, w_ref, res_ref, mod_ref, o_ref = refs[:5]
        scratch = list(refs[5:])
    else:
        a_ref, w_ref, o_ref = refs[:3]
        scratch = list(refs[3:])

    def finish(acc):
        if epilogue == "relu2":
            r = jnp.maximum(acc, 0.0)
            acc = r * r
        elif epilogue == "gated_res":
            acc = res_ref[...] + mod_ref[0, gate_row:gate_row + 1, :] * acc
        o_ref[...] = acc.astype(o_ref.dtype)

    if cast_w:
        w16_ref = scratch.pop(0)

        @pl.when(pl.program_id(1) == 0)
        def _():
            w16_ref[...] = w_ref[...].astype(BF16)

        w = w16_ref[...]
    else:
        w = w_ref[...]

    if nk == 1:
        finish(jnp.dot(a_ref[...], w, preferred_element_type=F32))
    else:
        acc_ref = scratch[0]
        k = pl.program_id(2)

        @pl.when(k == 0)
        def _():
            acc_ref[...] = jnp.zeros_like(acc_ref)

        acc_ref[...] += jnp.dot(a_ref[...], w, preferred_element_type=F32)

        @pl.when(k == nk - 1)
        def _():
            finish(acc_ref[...])


def _matmul(a, w, *, tm, tn, tk, out_dtype, epilogue="none", res=None, mod=None, gate_row=0,
            seq=None, n_batch=None, layer=None, n_cols=None, cast_w=False, rows_outer=False, name="matmul"):
    m, kdim = a.shape
    stacked = w.ndim == 3
    assert stacked or not cast_w
    n = n_cols if stacked else w.shape[1]
    assert m % tm == 0 and n % tn == 0 and kdim % tk == 0, (m, n, kdim, tm, tn, tk)
    nk = kdim // tk
    if rows_outer:
        assert not cast_w
        ij = lambda f: (lambda i, j, k: f(i, j, k))
        grid = (m // tm, n // tn, nk)
    else:
        ij = lambda f: (lambda j, i, k: f(i, j, k))
        grid = (n // tn, m // tm, nk)
    assert nk == 1 or not cast_w
    if stacked:
        w_spec = pl.BlockSpec((None, tk, tn), ij(lambda i, j, k: (layer, k, j)))
    else:
        w_spec = pl.BlockSpec((tk, tn), ij(lambda i, j, k: (k, j)))
    in_specs = [pl.BlockSpec((tm, tk), ij(lambda i, j, k: (i, k))), w_spec]
    args = [a, w]
    if epilogue == "gated_res":
        bps = seq // tm
        in_specs += [pl.BlockSpec((tm, tn), ij(lambda i, j, k: (i, j))),
                     pl.BlockSpec((1, N_MOD, tn), ij(lambda i, j, k: (jnp.minimum(i // bps, n_batch), 0, j)))]
        args += [res, mod]
    scratch = [pltpu.VMEM((tk, tn), BF16)] if cast_w else []
    scratch += [pltpu.VMEM((tm, tn), F32)] if nk > 1 else []
    return pl.pallas_call(
        functools.partial(_mm_kernel, nk=nk, epilogue=epilogue, gate_row=gate_row, cast_w=cast_w),
        out_shape=jax.ShapeDtypeStruct((m, n), out_dtype),
        grid=grid,
        in_specs=in_specs,
        out_specs=pl.BlockSpec((tm, tn), ij(lambda i, j, k: (i, j))),
        scratch_shapes=scratch,
        compiler_params=_params(("parallel", "arbitrary", "arbitrary")),
        name=name,
    )(*args)


def _na_band_shift(offset, variant):
    if isinstance(variant, int):
        return (max(offset, 0), offset, min(offset, 0))[variant]
    return jnp.where(variant == 0, max(offset, 0), jnp.where(variant == 2, min(offset, 0), offset))


def _na_kernel(q_ref, kp_ref, kc_ref, kn_ref, vp_ref, vc_ref, vn_ref, kx_ref, vx_ref,
               bias_ref, qw_ref, kw_ref, o_ref, k_scr, v_scr, *, nj):
    j = pl.program_id(2)
    variant = jnp.where(j == 0, 0, jnp.where(j == nj - 1, 2, 1))
    tq = q_ref.shape[0]
    qw = qw_ref[...]
    kw = kw_ref[...]
    for idx, (k_ref, v_ref) in enumerate(((kp_ref, vp_ref), (kc_ref, vc_ref), (kn_ref, vn_ref))):
        rows = slice(idx * tq, (idx + 1) * tq)
        k_scr[rows, :] = _rms(k_ref[...], kw).astype(BF16)
        v_scr[rows, :HEAD] = v_ref[...].astype(BF16)
        v_scr[rows, HEAD:] = jnp.ones((tq, HEAD), BF16)
    kx = _rms(kx_ref[...], kw).astype(BF16)
    vx = vx_ref[...].astype(BF16)
    vx1 = jnp.concatenate([vx, jnp.ones_like(vx)], axis=1)

    sub_q = NA_SUB_ROWS * GRID_W
    sub_k = (NA_WIN_H + NA_SUB_ROWS) * GRID_W
    subs = range(NA_ROWS // NA_SUB_ROWS)
    qs, kls, vls = [], [], []
    for s in subs:
        first_key_row = NA_ROWS + _na_band_shift(NA_SUB_ROWS * s - NA_WIN_H // 2, variant)
        k0 = pl.multiple_of(first_key_row * GRID_W, NA_SUB_ROWS * GRID_W)
        qs.append((_rms(q_ref[s * sub_q:(s + 1) * sub_q, :], qw) * (HEAD ** -0.5)).astype(BF16))
        kls.append(k_scr[pl.ds(k0, sub_k), :])
        vls.append(v_scr[pl.ds(k0, sub_k), :])
    s_loc = [_dot_t(qs[s], kls[s]) + bias_ref[0, 0, s] for s in subs]
    s_ctx = [_dot_t(qs[s], kx) for s in subs]
    mx = [jnp.maximum(jnp.max(s_loc[s], axis=-1, keepdims=True), jnp.max(s_ctx[s], axis=-1, keepdims=True))
          for s in subs]
    e_loc = [jnp.exp((s_loc[s] - mx[s]).astype(BF16)) for s in subs]
    e_ctx = [jnp.exp((s_ctx[s] - mx[s]).astype(BF16)) for s in subs]
    acc = [jnp.dot(e_loc[s], vls[s], preferred_element_type=F32) + jnp.dot(e_ctx[s], vx1, preferred_element_type=F32)
           for s in subs]
    for s in subs:
        o_ref[s * sub_q:(s + 1) * sub_q, :] = (acc[s][:, :HEAD] / acc[s][:, HEAD:HEAD + 1]).astype(o_ref.dtype)


def _na_bias_tables(rpb):
    h = rpb.shape[0]
    cq = jnp.arange(GRID_W)
    dc = jnp.clip(cq[None, :] - cq[:, None], -(NA_WIN_W - 1), NA_WIN_W - 1) + (NA_WIN_W - 1)
    col_start = jnp.clip(cq - NA_WIN_W // 2, 0, GRID_W - NA_WIN_W)
    col_ok = (cq[None, :] >= col_start[:, None]) & (cq[None, :] < col_start[:, None] + NA_WIN_W)
    t = jnp.where(col_ok, rpb[:, :, dc], NEG)
    band = jnp.stack([t[:, d0:d0 + NA_WIN_H] for d0 in range(NA_WIN_H)], axis=1)
    band = band.transpose(0, 1, 3, 2, 4).reshape(h, NA_WIN_H, GRID_W, NA_WIN_H * GRID_W)
    half = NA_WIN_H // 2
    sub_k = (NA_WIN_H + NA_SUB_ROWS) * GRID_W
    tabs = []
    for variant in range(3):
        subs = []
        for s in range(NA_ROWS // NA_SUB_ROWS):
            first = NA_ROWS + _na_band_shift(NA_SUB_ROWS * s - half, variant)
            slabs = []
            for rq in range(NA_SUB_ROWS * s, NA_SUB_ROWS * (s + 1)):
                start = NA_ROWS + _na_band_shift(rq - half, variant)
                left = (start - first) * GRID_W
                slabs.append(jnp.pad(band[:, start - rq - 1],
                                     ((0, 0), (0, 0), (left, sub_k - left - NA_WIN_H * GRID_W)), constant_values=NEG))
            subs.append(jnp.concatenate(slabs, axis=1))
        tabs.append(jnp.stack(subs, axis=1))
    return jnp.stack(tabs)


def _na_latent(u, bias, qw, kw, *, n_batch, seq, ctx_len, n_heads):
    tq = NA_ROWS * GRID_W
    nj = seq // tq
    assert nj >= 2 and seq % tq == 0
    kcol = n_heads
    vcol = 2 * n_heads
    ctx_blk = n_batch * seq // ctx_len
    qmap = lambda b, h, j: (b * nj + j, h)
    prev = lambda col: (lambda b, h, j: (b * nj + jnp.maximum(j - 1, 0), col + h))
    cur = lambda col: (lambda b, h, j: (b * nj + j, col + h))
    nxt = lambda col: (lambda b, h, j: (b * nj + jnp.minimum(j + 1, nj - 1), col + h))
    cx = lambda col: (lambda b, h, j: (ctx_blk + b, col + h))
    variant = lambda j: jnp.where(j == 0, 0, jnp.where(j == nj - 1, 2, 1))
    blk = lambda rows, imap: pl.BlockSpec((rows, HEAD), imap)
    return pl.pallas_call(
        functools.partial(_na_kernel, nj=nj),
        out_shape=jax.ShapeDtypeStruct((n_batch * seq, n_heads * HEAD), BF16),
        grid=(n_batch, n_heads, nj),
        in_specs=[blk(tq, qmap),
                  blk(tq, prev(kcol)), blk(tq, cur(kcol)), blk(tq, nxt(kcol)),
                  blk(tq, prev(vcol)), blk(tq, cur(vcol)), blk(tq, nxt(vcol)),
                  blk(ctx_len, cx(kcol)), blk(ctx_len, cx(vcol)),
                  pl.BlockSpec((1, 1) + bias.shape[2:], lambda b, h, j: (variant(j), h, 0, 0, 0)),
                  pl.BlockSpec((1, HEAD), lambda b, h, j: (0, 0)),
                  pl.BlockSpec((1, HEAD), lambda b, h, j: (0, 0))],
        out_specs=blk(tq, qmap),
        scratch_shapes=[pltpu.VMEM((3 * tq, HEAD), BF16), pltpu.VMEM((3 * tq, 2 * HEAD), BF16)],
        compiler_params=_params(("parallel", "parallel", "arbitrary")),
        name="na_latent",
    )(u, u, u, u, u, u, u, u, u, bias, qw.reshape(1, HEAD), kw.reshape(1, HEAD))


def _na_ctx_kernel(q_ref, k_ref, v_ref, qw_ref, kw_ref, o_ref):
    q = (_rms(q_ref[...], qw_ref[...]) * (HEAD ** -0.5)).astype(BF16)
    k = _rms(k_ref[...], kw_ref[...]).astype(BF16)
    s = _dot_t(q, k)
    e = jnp.exp(s - jnp.max(s, axis=-1, keepdims=True))
    o = jnp.dot(e.astype(BF16), v_ref[...].astype(BF16), preferred_element_type=F32)
    o_ref[...] = (o / jnp.sum(e, axis=-1, keepdims=True)).astype(o_ref.dtype)


def _na_context(u, qw, kw, *, n_batch, seq, ctx_len, n_heads):
    ctx_blk = n_batch * seq // ctx_len
    cx = lambda col: (lambda b, h: (ctx_blk + b, col + h))
    blk = lambda imap: pl.BlockSpec((ctx_len, HEAD), imap)
    return pl.pallas_call(
        _na_ctx_kernel,
        out_shape=jax.ShapeDtypeStruct((n_batch * ctx_len, n_heads * HEAD), BF16),
        grid=(n_batch, n_heads),
        in_specs=[blk(cx(0)), blk(cx(n_heads)), blk(cx(2 * n_heads)),
                  pl.BlockSpec((1, HEAD), lambda b, h: (0, 0)),
                  pl.BlockSpec((1, HEAD), lambda b, h: (0, 0))],
        out_specs=blk(lambda b, h: (b, h)),
        compiler_params=_params(("parallel", "parallel")),
        name="na_context",
    )(u, u, u, qw.reshape(1, HEAD), kw.reshape(1, HEAD))


def _gdn_conv_kernel(prev_ref, cur_ref, next_ref, w_ref, o_ref, *, lat_blocks, seq_blocks, ctx_blocks, n_heads):
    i = pl.program_id(0)
    j = pl.program_id(1)
    pos = jnp.where(i < lat_blocks, i % seq_blocks, (i - lat_blocks) % ctx_blocks)
    length = jnp.where(i < lat_blocks, seq_blocks, ctx_blocks)
    first = (pos == 0).astype(F32)
    last = (pos == length - 1).astype(F32)
    tr = cur_ref.shape[0]
    n_ext = tr + 16
    is_qk = j < 2
    scale = jnp.where(j == 0, HEAD ** -0.5, 1.0)
    for h in range(n_heads):
        sl = slice(h * HEAD, (h + 1) * HEAD)
        ext = jnp.concatenate([prev_ref[:, sl] * (1.0 - first), cur_ref[:, sl], next_ref[:, sl] * (1.0 - last)],
                              axis=0)
        acc = jnp.zeros((tr, HEAD), F32)
        for tap in range(SHORT_CONV):
            d = tap - SHORT_CONV // 2
            shifted = ext if d == 0 else pltpu.roll(ext, (-d) % n_ext, axis=0)
            acc = acc + shifted[8:8 + tr] * w_ref[tap:tap + 1, sl]
        y = acc * (1.0 / (1.0 + jnp.exp(-acc)))
        norm = lax.rsqrt(jnp.sum(y * y, axis=-1, keepdims=True) + EPS) * scale
        o_ref[:, sl] = y * jnp.where(is_qk, norm, 1.0)


def _gdn_conv(u, conv_w, *, n_batch, seq, ctx_len, gw):
    m = u.shape[0]
    tr = ROW_BLOCK
    n_heads = gw // HEAD
    lat_blocks = n_batch * seq // tr
    halo = tr // 8
    nblk8 = m // 8
    return pl.pallas_call(
        functools.partial(_gdn_conv_kernel, lat_blocks=lat_blocks, seq_blocks=seq // tr,
                          ctx_blocks=ctx_len // tr, n_heads=n_heads),
        out_shape=jax.ShapeDtypeStruct((m, 3 * gw), F32),
        grid=(m // tr, 3),
        in_specs=[pl.BlockSpec((8, gw), lambda i, j: (jnp.maximum(i * halo - 1, 0), 2 + j)),
                  pl.BlockSpec((tr, gw), lambda i, j: (i, 2 + j)),
                  pl.BlockSpec((8, gw), lambda i, j: (jnp.minimum((i + 1) * halo, nblk8 - 1), 2 + j)),
                  pl.BlockSpec((SHORT_CONV, gw), lambda i, j: (0, j))],
        out_specs=pl.BlockSpec((tr, gw), lambda i, j: (i, j)),
        compiler_params=_params(("parallel", "parallel")),
        name="gdn_conv",
    )(u, u, u, conv_w)


def _cumsum_rows(x, reverse):
    n = x.shape[0]
    row = lax.broadcasted_iota(jnp.int32, x.shape, 0)
    s = 1
    while s < n:
        if reverse:
            x = x + jnp.where(row < n - s, pltpu.roll(x, n - s, axis=0), 0.0)
        else:
            x = x + jnp.where(row >= s, pltpu.roll(x, s, axis=0), 0.0)
        s *= 2
    return x


def _unit_triangular_inverse(lmats, ij_xor):
    c = lmats[0].shape[0]
    eye = (ij_xor == 0).astype(F32)
    dot = functools.partial(jnp.dot, preferred_element_type=F32)
    xs = [eye - jnp.where(ij_xor < 2, lm, 0.0) for lm in lmats]
    blk = 2
    while blk < c:
        pick = (ij_xor // blk) == 1
        x16 = [x.astype(BF16) for x in xs]
        xc = [dot(x, jnp.where(pick, lm, 0.0).astype(BF16)) for x, lm in zip(x16, lmats)]
        xcx = [dot(y.astype(BF16), x) for y, x in zip(xc, x16)]
        xs = [x - z for x, z in zip(xs, xcx)]
        blk *= 2
    return xs


def _gdn_scan_kernel(qf_ref, kf_ref, vf_ref, gf_ref, qb_ref, kb_ref, vb_ref, gb_ref, a_ref, dt_ref,
                     of_ref, ob_ref, state_ref, *, n_heads):
    @pl.when(pl.program_id(1) == 0)
    def _():
        state_ref[...] = jnp.zeros_like(state_ref)

    c = GDN_CHUNK
    dot = functools.partial(jnp.dot, preferred_element_type=F32)
    ii = lax.broadcasted_iota(jnp.int32, (c, c), 0)
    jj = lax.broadcasted_iota(jnp.int32, (c, c), 1)
    ij_xor = ii ^ jj

    qs, ks, vs, betas, gcols, grows, egs, kes, cds, incls, stricts, outs = ([] for _ in range(12))
    for d, (q_ref, k_ref, v_ref, g_ref, o_ref) in enumerate(
            ((qf_ref, kf_ref, vf_ref, gf_ref, of_ref), (qb_ref, kb_ref, vb_ref, gb_ref, ob_ref))):
        reverse = d == 1
        raw = g_ref[...]
        beta_all = 1.0 / (1.0 + jnp.exp(-raw))
        z = raw + dt_ref[...]
        softplus = jnp.maximum(z, 0.0) + jnp.log1p(jnp.exp(-jnp.abs(z)))
        gcum = _cumsum_rows(a_ref[...] * softplus, reverse)
        gcum_t = gcum.T
        last_row = 0 if reverse else c - 1
        g_last = gcum[last_row:last_row + 1, :]
        eg_all = jnp.exp(gcum)
        ke_all = jnp.exp(g_last - gcum)
        cd_all = jnp.exp(g_last)
        incl = (jj >= ii) if reverse else (jj <= ii)
        strict = (jj > ii) if reverse else (jj < ii)
        for h in range(n_heads):
            bcol = d * n_heads + h
            gcol = 2 * n_heads + bcol
            sl = slice(h * HEAD, (h + 1) * HEAD)
            qs.append(q_ref[:, sl])
            ks.append(k_ref[:, sl])
            vs.append(v_ref[:, sl])
            betas.append(beta_all[:, bcol:bcol + 1])
            gcols.append(gcum[:, gcol:gcol + 1])
            grows.append(gcum_t[gcol:gcol + 1, :])
            egs.append(eg_all[:, gcol:gcol + 1])
            kes.append(ke_all[:, gcol:gcol + 1])
            cds.append(cd_all[:, gcol:gcol + 1])
            incls.append(incl)
            stricts.append(strict)
            outs.append((o_ref, sl))
    n = len(qs)
    rng = range(n)

    decay = [jnp.exp(jnp.where(incls[i], gcols[i] - grows[i], NEG)) for i in rng]
    kbs = [ks[i] * betas[i] for i in rng]
    kk = [_dot_t(jnp.concatenate([kbs[i], qs[i]], axis=0).astype(BF16), ks[i].astype(BF16)) for i in rng]
    lmats = [jnp.where(stricts[i], kk[i][:c] * decay[i], 0.0) for i in rng]
    amats = [(kk[i][c:] * decay[i]).astype(BF16) for i in rng]
    tinvs = _unit_triangular_inverse(lmats, ij_xor)
    rhs = [jnp.concatenate([vs[i] * betas[i], kbs[i] * egs[i]], axis=1).astype(BF16) for i in rng]
    uw = [dot(tinvs[i].astype(BF16), rhs[i]).astype(BF16) for i in rng]
    auw = [dot(amats[i], uw[i]) for i in rng]
    kuw = [lax.dot_general((ks[i] * kes[i]).astype(BF16), uw[i], (((0,), (0,)), ((), ())),
                           preferred_element_type=F32) for i in rng]
    q2 = [(qs[i] * egs[i] - auw[i][:, HEAD:]).astype(BF16) for i in rng]
    states = [state_ref[i] for i in rng]
    s16 = [st.astype(BF16) for st in states]
    out = [auw[i][:, :HEAD] + dot(q2[i], s16[i]) for i in rng]
    new_states = [states[i] * cds[i] + kuw[i][:, :HEAD] - dot(kuw[i][:, HEAD:].astype(BF16), s16[i]) for i in rng]
    for i in rng:
        o_ref, sl = outs[i]
        o_ref[:, sl] = out[i]
        state_ref[i] = new_states[i]


def _gdn_scan(qkv, graw, a_row, dt_row, *, n_batch, seq, ctx_len, gw):
    m = qkv.shape[0]
    c = GDN_CHUNK
    n_heads = gw // HEAD
    n_ctx = ctx_len // c
    n_lat = seq // c

    def row_block(b, t, d):
        tl = t - n_ctx
        c_ctx = t if d == 0 else n_ctx - 1 - t
        c_lat = tl if d == 0 else n_lat - 1 - tl
        return jnp.where(t < n_ctx, (n_batch * seq + b * ctx_len) // c + c_ctx, (b * seq) // c + c_lat)

    def spec(col, d):
        return pl.BlockSpec((c, gw), lambda b, t: (row_block(b, t, d), col))

    def gspec(d):
        return pl.BlockSpec((c, LANES), lambda b, t: (row_block(b, t, d), 0))

    const = pl.BlockSpec((1, LANES), lambda b, t: (0, 0))
    in_specs = []
    for d in range(2):
        in_specs += [spec(0, d), spec(1, d), spec(2, d), gspec(d)]
    in_specs += [const, const]
    args = [qkv, qkv, qkv, graw, qkv, qkv, qkv, graw, a_row, dt_row]
    return pl.pallas_call(
        functools.partial(_gdn_scan_kernel, n_heads=n_heads),
        out_shape=[jax.ShapeDtypeStruct((m, gw), F32), jax.ShapeDtypeStruct((m, gw), F32)],
        grid=(n_batch, n_ctx + n_lat),
        in_specs=in_specs,
        out_specs=[spec(0, 0), spec(0, 1)],
        scratch_shapes=[pltpu.VMEM((2 * n_heads, HEAD, HEAD), F32)],
        compiler_params=_params(("parallel", "arbitrary")),
        name="gdn_scan",
    )(*args)


def _gdn_out_kernel(of_ref, ob_ref, z_ref, w_ref, o_ref, *, n_heads):
    w = w_ref[...]
    for h in range(n_heads):
        sl = slice(h * HEAD, (h + 1) * HEAD)
        o = _rms(of_ref[:, sl] + ob_ref[:, sl], w)
        z = z_ref[:, sl]
        o_ref[:, sl] = (o * (z / (1.0 + jnp.exp(-z)))).astype(o_ref.dtype)


def _gdn_out(o_fwd, o_bwd, u, w, *, gw):
    m = o_fwd.shape[0]
    tr = ROW_BLOCK
    blk = lambda col: pl.BlockSpec((tr, gw), lambda i: (i, col))
    return pl.pallas_call(
        functools.partial(_gdn_out_kernel, n_heads=gw // HEAD),
        out_shape=jax.ShapeDtypeStruct((m, gw), BF16),
        grid=(m // tr,),
        in_specs=[blk(0), blk(0), blk(5), pl.BlockSpec((1, HEAD), lambda i: (0, 0))],
        out_specs=blk(0),
        compiler_params=_params(("parallel",)),
        name="gdn_out",
    )(o_fwd, o_bwd, u, w.reshape(1, HEAD))


def _diff_prep_kernel(x_ref, w_ref, cos_ref, sin_ref, o_ref, *, n_heads):
    j = pl.program_id(1)

    @pl.when(j < 2)
    def _():
        w = w_ref[0]
        cos = cos_ref[...]
        sin = sin_ref[...]
        lane = lax.broadcasted_iota(jnp.int32, (x_ref.shape[0], HEAD), 1)
        low = lane < DIFF_DIM
        first_half = (lane % DIFF_DIM) < DIFF_DIM // 2
        scale = jnp.where(j == 0, DIFF_DIM ** -0.5 * math.log2(math.e), 1.0)
        for h in range(n_heads):
            sl = slice(h * HEAD, (h + 1) * HEAD)
            x = x_ref[:, sl]
            sq = x * x
            s_low = jnp.sum(jnp.where(low, sq, 0.0), axis=-1, keepdims=True)
            s_high = jnp.sum(jnp.where(low, 0.0, sq), axis=-1, keepdims=True)
            ms = jnp.where(low, s_low, s_high) * (1.0 / DIFF_DIM)
            y = x * lax.rsqrt(ms + EPS) * w
            swapped = jnp.where(first_half, pltpu.roll(y, HEAD - DIFF_DIM // 2, axis=1),
                                pltpu.roll(y, DIFF_DIM // 2, axis=1))
            o_ref[:, sl] = ((y * cos + swapped * sin) * scale).astype(o_ref.dtype)

    @pl.when(j == 2)
    def _():
        o_ref[...] = x_ref[...].astype(o_ref.dtype)


def _diff_prep(u, w_qk, cos_t, sin_t, *, n_batch, seq, gw):
    m = u.shape[0]
    tr = ROW_BLOCK
    lat_blocks = n_batch * seq // tr
    seq_blocks = seq // tr
    rope_map = lambda i, j: (jnp.where(i < lat_blocks, i % seq_blocks, seq_blocks), 0)
    return pl.pallas_call(
        functools.partial(_diff_prep_kernel, n_heads=gw // HEAD),
        out_shape=jax.ShapeDtypeStruct((m, 3 * gw), BF16),
        grid=(m // tr, 3),
        in_specs=[pl.BlockSpec((tr, gw), lambda i, j: (i, j)),
                  pl.BlockSpec((1, 1, HEAD), lambda i, j: (jnp.minimum(j, 1), 0, 0)),
                  pl.BlockSpec((tr, HEAD), rope_map),
                  pl.BlockSpec((tr, HEAD), rope_map)],
        out_specs=pl.BlockSpec((tr, gw), lambda i, j: (i, j)),
        compiler_params=_params(("parallel", "parallel")),
        name="diff_prep",
    )(u, w_qk, cos_t, sin_t)


def _diff_attn_kernel(*refs, n_seg):
    q_ref = refs[0]
    k_refs = refs[1:1 + n_seg]
    v_refs = refs[1 + n_seg:1 + 2 * n_seg]
    lam_ref, w_ref, o_ref = refs[1 + 2 * n_seg:]
    tq = q_ref.shape[0]
    ts = min(tq, DIFF_SUB_ROWS)
    subs = range(tq // ts)
    lane = lax.broadcasted_iota(jnp.int32, (ts, HEAD), 1)
    zero = jnp.zeros((ts, HEAD), q_ref.dtype)
    v1s = [jnp.concatenate([v_ref[...], jnp.ones(v_ref.shape, v_ref.dtype)], axis=1) for v_ref in v_refs]
    q2s = []
    for i in subs:
        q = q_ref[i * ts:(i + 1) * ts, :]
        q2s.append(jnp.concatenate([jnp.where(lane < DIFF_DIM, q, zero), jnp.where(lane < DIFF_DIM, zero, q)], axis=0))
    scores = [[_dot_t(q2, k_ref[...]) for k_ref in k_refs] for q2 in q2s]
    mxs = [functools.reduce(jnp.maximum, [jnp.max(s, axis=-1, keepdims=True) for s in sc]) for sc in scores]
    es = [[jnp.exp2((s - mx).astype(BF16)) for s in sc] for sc, mx in zip(scores, mxs)]
    accs = [functools.reduce(jnp.add, [jnp.dot(e, v1, preferred_element_type=F32) for e, v1 in zip(e_seg, v1s)])
            for e_seg in es]
    for i, acc in zip(subs, accs):
        r = acc[:, :HEAD] / acc[:, HEAD:HEAD + 1]
        o = r[:ts] - lam_ref[0:1, 0:1] * r[ts:]
        o_ref[i * ts:(i + 1) * ts, :] = _rms(o, w_ref[...]).astype(o_ref.dtype)


def _diff_attention(dp, lam_row, w_sub, *, n_batch, seq, ctx_len, n_heads, latent):
    kcol = n_heads
    vcol = 2 * n_heads
    ctx_blk = n_batch * seq // ctx_len
    if latent:
        tq = min(seq, DIFF_Q_ROWS)
        nq = seq // tq
        qmap = lambda b, h, i: (b * nq + i, h)
        segs = [(seq, lambda col: (lambda b, h, i: (b, col + h))),
                (ctx_len, lambda col: (lambda b, h, i: (ctx_blk + b, col + h)))]
        out_rows = n_batch * seq
        omap = qmap
    else:
        tq = ctx_len
        nq = 1
        qmap = lambda b, h, i: (ctx_blk + b, h)
        segs = [(ctx_len, lambda col: (lambda b, h, i: (ctx_blk + b, col + h)))]
        out_rows = n_batch * ctx_len
        omap = lambda b, h, i: (b, h)
    in_specs = [pl.BlockSpec((tq, HEAD), qmap)]
    in_specs += [pl.BlockSpec((rows, HEAD), mk(kcol)) for rows, mk in segs]
    in_specs += [pl.BlockSpec((rows, HEAD), mk(vcol)) for rows, mk in segs]
    in_specs += [pl.BlockSpec((1, HEAD), lambda b, h, i: (0, 0))] * 2
    n_seg = len(segs)
    return pl.pallas_call(
        functools.partial(_diff_attn_kernel, n_seg=n_seg),
        out_shape=jax.ShapeDtypeStruct((out_rows, n_heads * HEAD), BF16),
        grid=(n_batch, n_heads, nq),
        in_specs=in_specs,
        out_specs=pl.BlockSpec((tq, HEAD), omap),
        compiler_params=_params(("parallel", "parallel", "arbitrary")),
        name="diff_attn_latent" if latent else "diff_attn_context",
    )(*([dp] * (1 + 2 * n_seg)), lam_row, w_sub)


def _rope_tables(seq, extra_rows):
    n_freq = DIFF_DIM // 4
    inv = ROPE_BASE ** (-jnp.arange(n_freq, dtype=F32) / n_freq)
    t = jnp.arange(seq)
    row = (t // GRID_W).astype(F32)
    col = (t % GRID_W).astype(F32)
    ang = jnp.concatenate([row[:, None] * inv, col[:, None] * inv], axis=-1)
    cos, sin = jnp.cos(ang), jnp.sin(ang)
    cos_t = jnp.tile(cos, (1, HEAD // cos.shape[1]))
    sin_t = jnp.tile(jnp.concatenate([-sin, sin], axis=-1), (1, HEAD // (2 * sin.shape[1])))
    cos_t = jnp.concatenate([cos_t, jnp.ones((extra_rows, HEAD), F32)], axis=0)
    sin_t = jnp.concatenate([sin_t, jnp.zeros((extra_rows, HEAD), F32)], axis=0)
    return cos_t, sin_t


def _kernel_impl(x, c, ctx, c_ctx, ada_down, ada_up, ada_bias, norm1, w_in, na_q_norm, na_k_norm, na_rpb,
                 gdn_conv, gdn_a_log, gdn_dt_bias, gdn_norm, diff_q_norm, diff_k_norm, diff_lambda, diff_subln,
                 w_out, norm2, w_ff1, w_ff2):
    n_batch, seq, d = x.shape
    ctx_len = ctx.shape[1]
    depth = w_in.shape[0]
    d_ff = w_ff1.shape[2]
    naw = d // 4
    gw = 3 * d // 8
    na_heads = naw // HEAD
    g_heads = gw // HEAD
    n_gate = 4 * g_heads
    m_lat = n_batch * seq
    m_all = m_lat + n_batch * ctx_len
    n_main = 9 * gw
    gate_off = 3 * naw + 4 * gw
    assert w_in.shape[2] == n_main + n_gate and n_gate <= LANES
    assert seq % (NA_ROWS * GRID_W) == 0 and ctx_len % ROW_BLOCK == 0 and seq % ctx_len == 0

    xs = jnp.concatenate([x.reshape(m_lat, d), ctx.reshape(n_batch * ctx_len, d)], axis=0)
    cond = jnp.concatenate([c, c_ctx[None, :]], axis=0)
    cond = cond * (1.0 / (1.0 + jnp.exp(-cond)))
    cond = jnp.pad(cond, ((0, 16 - (n_batch + 1)), (0, 0))).astype(BF16)
    cos_t, sin_t = _rope_tables(seq, ROW_BLOCK)
    dims = dict(n_batch=n_batch, seq=seq, ctx_len=ctx_len)
    tm = 512
    w_in16 = w_in.astype(BF16)
    w_out16 = w_out.astype(BF16)
    w_ff2_16 = w_ff2.astype(BF16)

    for l in range(depth):
        last = l == depth - 1
        lam_init = 0.8 - 0.6 * math.exp(-0.3 * l)
        w_diff = w_in16[l][:, gate_off + n_gate:]
        w_gate = jnp.pad(w_in16[l][:, gate_off:gate_off + n_gate], ((0, 0), (0, LANES - n_gate)))
        a_row = jnp.pad(-jnp.exp(gdn_a_log[l]).reshape(1, -1), ((0, 0), (2 * g_heads, LANES - n_gate)))
        dt_row = jnp.pad(gdn_dt_bias[l].reshape(1, -1), ((0, 0), (2 * g_heads, LANES - n_gate)))
        lv = diff_lambda[l]
        lam = jnp.exp(jnp.sum(lv[0] * lv[1])) - jnp.exp(jnp.sum(lv[2] * lv[3])) + lam_init
        lam_row = jnp.full((1, HEAD), lam, F32)
        w_sub = (diff_subln[l] * (1.0 - lam_init)).reshape(1, HEAD)
        w_qk = jnp.stack([jnp.tile(diff_q_norm[l], 2), jnp.tile(diff_k_norm[l], 2)]).reshape(2, 1, HEAD)
        bias = _na_bias_tables(na_rpb[l])

        low = _matmul(cond, ada_down[l].astype(BF16), tm=16, tn=ada_down.shape[2], tk=d, out_dtype=BF16,
                      name="ada_down")
        mod = _matmul(low, ada_up[l].astype(BF16), tm=16, tn=_pick_tile(N_MOD * d, 4096), tk=ada_down.shape[2],
                      out_dtype=F32, name="ada_up")
        mod = (mod[:n_batch + 1] + ada_bias[l][None, :]).reshape(n_batch + 1, N_MOD, d)

        h1 = _modulate(xs, norm1[l], mod, 0, seq=seq, n_batch=n_batch)
        u = _matmul(h1, w_in16, layer=l, n_cols=gate_off, tm=tm, tn=_pick_tile(gate_off, 1536), tk=d,
                    out_dtype=F32, name="w_in")
        u_diff = _matmul(h1, w_diff, tm=tm, tn=_pick_tile(3 * gw, 1536), tk=d, out_dtype=F32, name="w_in_diff")
        graw = _matmul(h1, w_gate, tm=tm, tn=LANES, tk=d, out_dtype=F32, name="w_in_gates")

        na_lat = _na_latent(u, bias, na_q_norm[l], na_k_norm[l], n_heads=na_heads, **dims)
        qkv = _gdn_conv(u, gdn_conv[l], gw=gw, **dims)
        o_fwd, o_bwd = _gdn_scan(qkv, graw, a_row, dt_row, gw=gw, **dims)
        gdn = _gdn_out(o_fwd, o_bwd, u, gdn_norm[l], gw=gw)
        dp = _diff_prep(u_diff, w_qk, cos_t, sin_t, n_batch=n_batch, seq=seq, gw=gw)
        df_lat = _diff_attention(dp, lam_row, w_sub, n_heads=g_heads, latent=True, **dims)

        if last:
            mix = jnp.concatenate([na_lat, gdn[:m_lat], df_lat], axis=1)
            xs = xs[:m_lat]
        else:
            na_ctx = _na_context(u, na_q_norm[l], na_k_norm[l], n_heads=na_heads, **dims)
            df_ctx = _diff_attention(dp, lam_row, w_sub, n_heads=g_heads, latent=False, **dims)
            mix = jnp.concatenate([jnp.concatenate([na_lat, na_ctx], axis=0), gdn,
                                   jnp.concatenate([df_lat, df_ctx], axis=0)], axis=1)

        xs = _matmul(mix, w_out16, layer=l, n_cols=d, tm=tm, tn=_pick_tile(d, 1024), tk=d, out_dtype=F32,
                     epilogue="gated_res", res=xs, mod=mod, gate_row=2, seq=seq, n_batch=n_batch, name="w_out")
        h2 = _modulate(xs, norm2[l], mod, 3, seq=seq, n_batch=n_batch)
        ff = _matmul(h2, w_ff1, layer=l, n_cols=d_ff, cast_w=True, tm=tm, tn=_pick_tile(d_ff, W_CAST_TILE), tk=d,
                     out_dtype=BF16, epilogue="relu2", name="w_ff1")
        xs = _matmul(ff, w_ff2_16, layer=l, n_cols=d, tm=tm, tn=2 * LANES, tk=d_ff, rows_outer=True, out_dtype=F32,
                     epilogue="gated_res", res=xs, mod=mod, gate_row=5, seq=seq, n_batch=n_batch, name="w_ff2")

    return xs[:m_lat].reshape(n_batch, seq, d)


@jax.jit
def kernel(x, c, ctx, c_ctx, ada_down, ada_up, ada_bias, norm1, w_in, na_q_norm, na_k_norm, na_rpb,
           gdn_conv, gdn_a_log, gdn_dt_bias, gdn_norm, diff_q_norm, diff_k_norm, diff_lambda, diff_subln,
           w_out, norm2, w_ff1, w_ff2):
    return _kernel_impl(x, c, ctx, c_ctx, ada_down, ada_up, ada_bias, norm1, w_in, na_q_norm, na_k_norm, na_rpb,
                        gdn_conv, gdn_a_log, gdn_dt_bias, gdn_norm, diff_q_norm, diff_k_norm, diff_lambda,
                        diff_subln, w_out, norm2, w_ff1, w_ff2)
```

```python
import functools
import math

import jax
import jax.numpy as jnp
from jax import lax
from jax.experimental import pallas as pl
from jax.experimental.pallas import tpu as pltpu

GRID_W = 64
EPS = 1e-6
N_MOD = 6
HEAD = 128
NA_WIN_H = 8
NA_WIN_W = 16
GDN_CHUNK = 64
SHORT_CONV = 5
DIFF_DIM = 64
ROPE_BASE = 10000.0
NEG = -1e30

LANES = 128
ROW_BLOCK = 256
NA_ROWS = 8
NA_SUB_ROWS = 2
W_CAST_TILE = 1024
DIFF_Q_ROWS = 512
DIFF_SUB_ROWS = 128
VMEM_LIMIT = 56 * 1024 * 1024

F32 = jnp.float32
BF16 = jnp.bfloat16


def _params(sem, vmem=VMEM_LIMIT):
    return pltpu.CompilerParams(dimension_semantics=sem, vmem_limit_bytes=vmem)


def _pick_tile(n, target):
    best = None
    for t in range(LANES, min(n, target) + 1, LANES):
        if n % t == 0:
            best = t
    assert best is not None, (n, target)
    return best


def _dot_t(a, b):
    return lax.dot_general(a, b, (((1,), (1,)), ((), ())), preferred_element_type=F32)


def _rms(x, w):
    return x * lax.rsqrt(jnp.mean(x * x, axis=-1, keepdims=True) + EPS) * w


def _modulate_kernel(x_ref, w_ref, mod_ref, o_ref, *, shift_row):
    x = x_ref[...]
    y = _rms(x, w_ref[...])
    shift = mod_ref[0, shift_row:shift_row + 1, :]
    scale = mod_ref[0, shift_row + 1:shift_row + 2, :]
    o_ref[...] = (y * (1.0 + scale) + shift).astype(o_ref.dtype)


def _modulate(x, w, mod, shift_row, *, seq, n_batch):
    m, d = x.shape
    tr = ROW_BLOCK
    bps = seq // tr
    seg = lambda i: jnp.minimum(i // bps, n_batch)
    return pl.pallas_call(
        functools.partial(_modulate_kernel, shift_row=shift_row),
        out_shape=jax.ShapeDtypeStruct((m, d), BF16),
        grid=(m // tr,),
        in_specs=[pl.BlockSpec((tr, d), lambda i: (i, 0)),
                  pl.BlockSpec((1, d), lambda i: (0, 0)),
                  pl.BlockSpec((1, N_MOD, d), lambda i: (seg(i), 0, 0))],
        out_specs=pl.BlockSpec((tr, d), lambda i: (i, 0)),
        compiler_params=_params(("parallel",)),
        name="modulate",
    )(x, w.reshape(1, d), mod)


def _mm_kernel(*refs, nk, epilogue, gate_row, cast_w):
    if epilogue == "gated_res":
        a_ref, w_ref, res_ref, mod_ref, o_ref = refs[:5]
        scratch = list(refs[5:])
    else:
        a_ref, w_ref, o_ref = refs[:3]
        scratch = list(refs[3:])

    def finish(acc):
        if epilogue == "relu2":
            r = jnp.maximum(acc, 0.0)
            acc = r * r
        elif epilogue == "gated_res":
            acc = res_ref[...] + mod_ref[0, gate_row:gate_row + 1, :] * acc
        o_ref[...] = acc.astype(o_ref.dtype)

    if cast_w:
        w16_ref = scratch.pop(0)

        @pl.when(pl.program_id(1) == 0)
        def _():
            w16_ref[...] = w_ref[...].astype(BF16)

        w = w16_ref[...]
    else:
        w = w_ref[...]

    if nk == 1:
        finish(jnp.dot(a_ref[...], w, preferred_element_type=F32))
    else:
        acc_ref = scratch[0]
        k = pl.program_id(2)

        @pl.when(k == 0)
        def _():
            acc_ref[...] = jnp.zeros_like(acc_ref)

        acc_ref[...] += jnp.dot(a_ref[...], w, preferred_element_type=F32)

        @pl.when(k == nk - 1)
        def _():
            finish(acc_ref[...])


def _matmul(a, w, *, tm, tn, tk, out_dtype, epilogue="none", res=None, mod=None, gate_row=0,
            seq=None, n_batch=None, layer=None, n_cols=None, cast_w=False, rows_outer=False, name="matmul"):
    m, kdim = a.shape
    stacked = w.ndim == 3
    assert stacked or not cast_w
    n = n_cols if stacked else w.shape[1]
    assert m % tm == 0 and n % tn == 0 and kdim % tk == 0, (m, n, kdim, tm, tn, tk)
    nk = kdim // tk
    if rows_outer:
        assert not cast_w
        ij = lambda f: (lambda i, j, k: f(i, j, k))
        grid = (m // tm, n // tn, nk)
    else:
        ij = lambda f: (lambda j, i, k: f(i, j, k))
        grid = (n // tn, m // tm, nk)
    assert nk == 1 or not cast_w
    if stacked:
        w_spec = pl.BlockSpec((None, tk, tn), ij(lambda i, j, k: (layer, k, j)))
    else:
        w_spec = pl.BlockSpec((tk, tn), ij(lambda i, j, k: (k, j)))
    in_specs = [pl.BlockSpec((tm, tk), ij(lambda i, j, k: (i, k))), w_spec]
    args = [a, w]
    if epilogue == "gated_res":
        bps = seq // tm
        in_specs += [pl.BlockSpec((tm, tn), ij(lambda i, j, k: (i, j))),
                     pl.BlockSpec((1, N_MOD, tn), ij(lambda i, j, k: (jnp.minimum(i // bps, n_batch), 0, j)))]
        args += [res, mod]
    scratch = [pltpu.VMEM((tk, tn), BF16)] if cast_w else []
    scratch += [pltpu.VMEM((tm, tn), F32)] if nk > 1 else []
    return pl.pallas_call(
        functools.partial(_mm_kernel, nk=nk, epilogue=epilogue, gate_row=gate_row, cast_w=cast_w),
        out_shape=jax.ShapeDtypeStruct((m, n), out_dtype),
        grid=grid,
        in_specs=in_specs,
        out_specs=pl.BlockSpec((tm, tn), ij(lambda i, j, k: (i, j))),
        scratch_shapes=scratch,
        compiler_params=_params(("parallel", "arbitrary", "arbitrary")),
        name=name,
    )(*args)


def _na_band_shift(offset, variant):
    if isinstance(variant, int):
        return (max(offset, 0), offset, min(offset, 0))[variant]
    return jnp.where(variant == 0, max(offset, 0), jnp.where(variant == 2, min(offset, 0), offset))


def _na_kernel(q_ref, kp_ref, kc_ref, kn_ref, vp_ref, vc_ref, vn_ref, kx_ref, vx_ref,
               bias_ref, qw_ref, kw_ref, o_ref, k_scr, v_scr, *, nj):
    j = pl.program_id(2)
    variant = jnp.where(j == 0, 0, jnp.where(j == nj - 1, 2, 1))
    tq = q_ref.shape[0]
    qw = qw_ref[...]
    kw = kw_ref[...]
    for idx, (k_ref, v_ref) in enumerate(((kp_ref, vp_ref), (kc_ref, vc_ref), (kn_ref, vn_ref))):
        rows = slice(idx * tq, (idx + 1) * tq)
        k_scr[rows, :] = _rms(k_ref[...], kw).astype(BF16)
        v_scr[rows, :HEAD] = v_ref[...].astype(BF16)
        v_scr[rows, HEAD:] = jnp.ones((tq, HEAD), BF16)
    kx = _rms(kx_ref[...], kw).astype(BF16)
    vx = vx_ref[...].astype(BF16)
    vx1 = jnp.concatenate([vx, jnp.ones_like(vx)], axis=1)

    sub_q = NA_SUB_ROWS * GRID_W
    sub_k = (NA_WIN_H + NA_SUB_ROWS) * GRID_W
    subs = range(NA_ROWS // NA_SUB_ROWS)
    qs, kls, vls = [], [], []
    for s in subs:
        first_key_row = NA_ROWS + _na_band_shift(NA_SUB_ROWS * s - NA_WIN_H // 2, variant)
        k0 = pl.multiple_of(first_key_row * GRID_W, NA_SUB_ROWS * GRID_W)
        qs.append((_rms(q_ref[s * sub_q:(s + 1) * sub_q, :], qw) * (HEAD ** -0.5)).astype(BF16))
        kls.append(k_scr[pl.ds(k0, sub_k), :])
        vls.append(v_scr[pl.ds(k0, sub_k), :])
    s_loc = [_dot_t(qs[s], kls[s]) + bias_ref[0, 0, s] for s in subs]
    s_ctx = [_dot_t(qs[s], kx) for s in subs]
    mx = [jnp.maximum(jnp.max(s_loc[s], axis=-1, keepdims=True), jnp.max(s_ctx[s], axis=-1, keepdims=True))
          for s in subs]
    e_loc = [jnp.exp((s_loc[s] - mx[s]).astype(BF16)) for s in subs]
    e_ctx = [jnp.exp((s_ctx[s] - mx[s]).astype(BF16)) for s in subs]
    acc = [jnp.dot(e_loc[s], vls[s], preferred_element_type=F32) + jnp.dot(e_ctx[s], vx1, preferred_element_type=F32)
           for s in subs]
    for s in subs:
        o_ref[s * sub_q:(s + 1) * sub_q, :] = (acc[s][:, :HEAD] / acc[s][:, HEAD:HEAD + 1]).astype(o_ref.dtype)


def _na_bias_tables(rpb):
    h = rpb.shape[0]
    cq = jnp.arange(GRID_W)
    dc = jnp.clip(cq[None, :] - cq[:, None], -(NA_WIN_W - 1), NA_WIN_W - 1) + (NA_WIN_W - 1)
    col_start = jnp.clip(cq - NA_WIN_W // 2, 0, GRID_W - NA_WIN_W)
    col_ok = (cq[None, :] >= col_start[:, None]) & (cq[None, :] < col_start[:, None] + NA_WIN_W)
    t = jnp.where(col_ok, rpb[:, :, dc], NEG)
    band = jnp.stack([t[:, d0:d0 + NA_WIN_H] for d0 in range(NA_WIN_H)], axis=1)
    band = band.transpose(0, 1, 3, 2, 4).reshape(h, NA_WIN_H, GRID_W, NA_WIN_H * GRID_W)
    half = NA_WIN_H // 2
    sub_k = (NA_WIN_H + NA_SUB_ROWS) * GRID_W
    tabs = []
    for variant in range(3):
        subs = []
        for s in range(NA_ROWS // NA_SUB_ROWS):
            first = NA_ROWS + _na_band_shift(NA_SUB_ROWS * s - half, variant)
            slabs = []
            for rq in range(NA_SUB_ROWS * s, NA_SUB_ROWS * (s + 1)):
                start = NA_ROWS + _na_band_shift(rq - half, variant)
                left = (start - first) * GRID_W
                slabs.append(jnp.pad(band[:, start - rq - 1],
                                     ((0, 0), (0, 0), (left, sub_k - left - NA_WIN_H * GRID_W)), constant_values=NEG))
            subs.append(jnp.concatenate(slabs, axis=1))
        tabs.append(jnp.stack(subs, axis=1))
    return jnp.stack(tabs)


def _na_latent(u, bias, qw, kw, *, n_batch, seq, ctx_len, n_heads):
    tq = NA_ROWS * GRID_W
    nj = seq // tq
    assert nj >= 2 and seq % tq == 0
    kcol = n_heads
    vcol = 2 * n_heads
    ctx_blk = n_batch * seq // ctx_len
    qmap = lambda b, h, j: (b * nj + j, h)
    prev = lambda col: (lambda b, h, j: (b * nj + jnp.maximum(j - 1, 0), col + h))
    cur = lambda col: (lambda b, h, j: (b * nj + j, col + h))
    nxt = lambda col: (lambda b, h, j: (b * nj + jnp.minimum(j + 1, nj - 1), col + h))
    cx = lambda col: (lambda b, h, j: (ctx_blk + b, col + h))
    variant = lambda j: jnp.where(j == 0, 0, jnp.where(j == nj - 1, 2, 1))
    blk = lambda rows, imap: pl.BlockSpec((rows, HEAD), imap)
    return pl.pallas_call(
        functools.partial(_na_kernel, nj=nj),
        out_shape=jax.ShapeDtypeStruct((n_batch * seq, n_heads * HEAD), BF16),
        grid=(n_batch, n_heads, nj),
        in_specs=[blk(tq, qmap),
                  blk(tq, prev(kcol)), blk(tq, cur(kcol)), blk(tq, nxt(kcol)),
                  blk(tq, prev(vcol)), blk(tq, cur(vcol)), blk(tq, nxt(vcol)),
                  blk(ctx_len, cx(kcol)), blk(ctx_len, cx(vcol)),
                  pl.BlockSpec((1, 1) + bias.shape[2:], lambda b, h, j: (variant(j), h, 0, 0, 0)),
                  pl.BlockSpec((1, HEAD), lambda b, h, j: (0, 0)),
                  pl.BlockSpec((1, HEAD), lambda b, h, j: (0, 0))],
        out_specs=blk(tq, qmap),
        scratch_shapes=[pltpu.VMEM((3 * tq, HEAD), BF16), pltpu.VMEM((3 * tq, 2 * HEAD), BF16)],
        compiler_params=_params(("parallel", "parallel", "arbitrary")),
        name="na_latent",
    )(u, u, u, u, u, u, u, u, u, bias, qw.reshape(1, HEAD), kw.reshape(1, HEAD))


def _na_ctx_kernel(q_ref, k_ref, v_ref, qw_ref, kw_ref, o_ref):
    q = (_rms(q_ref[...], qw_ref[...]) * (HEAD ** -0.5)).astype(BF16)
    k = _rms(k_ref[...], kw_ref[...]).astype(BF16)
    s = _dot_t(q, k)
    e = jnp.exp(s - jnp.max(s, axis=-1, keepdims=True))
    o = jnp.dot(e.astype(BF16), v_ref[...].astype(BF16), preferred_element_type=F32)
    o_ref[...] = (o / jnp.sum(e, axis=-1, keepdims=True)).astype(o_ref.dtype)


def _na_context(u, qw, kw, *, n_batch, seq, ctx_len, n_heads):
    ctx_blk = n_batch * seq // ctx_len
    cx = lambda col: (lambda b, h: (ctx_blk + b, col + h))
    blk = lambda imap: pl.BlockSpec((ctx_len, HEAD), imap)
    return pl.pallas_call(
        _na_ctx_kernel,
        out_shape=jax.ShapeDtypeStruct((n_batch * ctx_len, n_heads * HEAD), BF16),
        grid=(n_batch, n_heads),
        in_specs=[blk(cx(0)), blk(cx(n_heads)), blk(cx(2 * n_heads)),
                  pl.BlockSpec((1, HEAD), lambda b, h: (0, 0)),
                  pl.BlockSpec((1, HEAD), lambda b, h: (0, 0))],
        out_specs=blk(lambda b, h: (b, h)),
        compiler_params=_params(("parallel", "parallel")),
        name="na_context",
    )(u, u, u, qw.reshape(1, HEAD), kw.reshape(1, HEAD))


def _gdn_conv_kernel(prev_ref, cur_ref, next_ref, w_ref, o_ref, *, lat_blocks, seq_blocks, ctx_blocks, n_heads):
    i = pl.program_id(0)
    j = pl.program_id(1)
    pos = jnp.where(i < lat_blocks, i % seq_blocks, (i - lat_blocks) % ctx_blocks)
    length = jnp.where(i < lat_blocks, seq_blocks, ctx_blocks)
    first = (pos == 0).astype(F32)
    last = (pos == length - 1).astype(F32)
    tr = cur_ref.shape[0]
    n_ext = tr + 16
    is_qk = j < 2
    scale = jnp.where(j == 0, HEAD ** -0.5, 1.0)
    for h in range(n_heads):
        sl = slice(h * HEAD, (h + 1) * HEAD)
        ext = jnp.concatenate([prev_ref[:, sl] * (1.0 - first), cur_ref[:, sl], next_ref[:, sl] * (1.0 - last)],
                              axis=0)
        acc = jnp.zeros((tr, HEAD), F32)
        for tap in range(SHORT_CONV):
            d = tap - SHORT_CONV // 2
            shifted = ext if d == 0 else pltpu.roll(ext, (-d) % n_ext, axis=0)
            acc = acc + shifted[8:8 + tr] * w_ref[tap:tap + 1, sl]
        y = acc * (1.0 / (1.0 + jnp.exp(-acc)))
        norm = lax.rsqrt(jnp.sum(y * y, axis=-1, keepdims=True) + EPS) * scale
        o_ref[:, sl] = y * jnp.where(is_qk, norm, 1.0)


def _gdn_conv(u, conv_w, *, n_batch, seq, ctx_len, gw):
    m = u.shape[0]
    tr = ROW_BLOCK
    n_heads = gw // HEAD
    lat_blocks = n_batch * seq // tr
    halo = tr // 8
    nblk8 = m // 8
    return pl.pallas_call(
        functools.partial(_gdn_conv_kernel, lat_blocks=lat_blocks, seq_blocks=seq // tr,
                          ctx_blocks=ctx_len // tr, n_heads=n_heads),
        out_shape=jax.ShapeDtypeStruct((m, 3 * gw), F32),
        grid=(m // tr, 3),
        in_specs=[pl.BlockSpec((8, gw), lambda i, j: (jnp.maximum(i * halo - 1, 0), 2 + j)),
                  pl.BlockSpec((tr, gw), lambda i, j: (i, 2 + j)),
                  pl.BlockSpec((8, gw), lambda i, j: (jnp.minimum((i + 1) * halo, nblk8 - 1), 2 + j)),
                  pl.BlockSpec((SHORT_CONV, gw), lambda i, j: (0, j))],
        out_specs=pl.BlockSpec((tr, gw), lambda i, j: (i, j)),
        compiler_params=_params(("parallel", "parallel")),
        name="gdn_conv",
    )(u, u, u, conv_w)


def _cumsum_rows(x, reverse):
    n = x.shape[0]
    row = lax.broadcasted_iota(jnp.int32, x.shape, 0)
    s = 1
    while s < n:
        if reverse:
            x = x + jnp.where(row < n - s, pltpu.roll(x, n - s, axis=0), 0.0)
        else:
            x = x + jnp.where(row >= s, pltpu.roll(x, s, axis=0), 0.0)
        s *= 2
    return x


def _unit_triangular_inverse(lmats, ij_xor):
    c = lmats[0].shape[0]
    eye = (ij_xor == 0).astype(F32)
    dot = functools.partial(jnp.dot, preferred_element_type=F32)
    xs = [eye - jnp.where(ij_xor < 2, lm, 0.0) for lm in lmats]
    blk = 2
    while blk < c:
        pick = (ij_xor // blk) == 1
        x16 = [x.astype(BF16) for x in xs]
        xc = [dot(x, jnp.where(pick, lm, 0.0).astype(BF16)) for x, lm in zip(x16, lmats)]
        xcx = [dot(y.astype(BF16), x) for y, x in zip(xc, x16)]
        xs = [x - z for x, z in zip(xs, xcx)]
        blk *= 2
    return xs


def _gdn_scan_kernel(qf_ref, kf_ref, vf_ref, gf_ref, qb_ref, kb_ref, vb_ref, gb_ref, a_ref, dt_ref,
                     of_ref, ob_ref, state_ref, *, n_heads):
    @pl.when(pl.program_id(1) == 0)
    def _():
        state_ref[...] = jnp.zeros_like(state_ref)

    c = GDN_CHUNK
    dot = functools.partial(jnp.dot, preferred_element_type=F32)
    ii = lax.broadcasted_iota(jnp.int32, (c, c), 0)
    jj = lax.broadcasted_iota(jnp.int32, (c, c), 1)
    ij_xor = ii ^ jj

    qs, ks, vs, betas, gcols, grows, egs, kes, cds, incls, stricts, outs = ([] for _ in range(12))
    for d, (q_ref, k_ref, v_ref, g_ref, o_ref) in enumerate(
            ((qf_ref, kf_ref, vf_ref, gf_ref, of_ref), (qb_ref, kb_ref, vb_ref, gb_ref, ob_ref))):
        reverse = d == 1
        raw = g_ref[...]
        beta_all = 1.0 / (1.0 + jnp.exp(-raw))
        z = raw + dt_ref[...]
        softplus = jnp.maximum(z, 0.0) + jnp.log1p(jnp.exp(-jnp.abs(z)))
        gcum = _cumsum_rows(a_ref[...] * softplus, reverse)
        gcum_t = gcum.T
        last_row = 0 if reverse else c - 1
        g_last = gcum[last_row:last_row + 1, :]
        eg_all = jnp.exp(gcum)
        ke_all = jnp.exp(g_last - gcum)
        cd_all = jnp.exp(g_last)
        incl = (jj >= ii) if reverse else (jj <= ii)
        strict = (jj > ii) if reverse else (jj < ii)
        for h in range(n_heads):
            bcol = d * n_heads + h
            gcol = 2 * n_heads + bcol
            sl = slice(h * HEAD, (h + 1) * HEAD)
            qs.append(q_ref[:, sl])
            ks.append(k_ref[:, sl])
            vs.append(v_ref[:, sl])
            betas.append(beta_all[:, bcol:bcol + 1])
            gcols.append(gcum[:, gcol:gcol + 1])
            grows.append(gcum_t[gcol:gcol + 1, :])
            egs.append(eg_all[:, gcol:gcol + 1])
            kes.append(ke_all[:, gcol:gcol + 1])
            cds.append(cd_all[:, gcol:gcol + 1])
            incls.append(incl)
            stricts.append(strict)
            outs.append((o_ref, sl))
    n = len(qs)
    rng = range(n)

    decay = [jnp.exp(jnp.where(incls[i], gcols[i] - grows[i], NEG)) for i in rng]
    kbs = [ks[i] * betas[i] for i in rng]
    kk = [_dot_t(jnp.concatenate([kbs[i], qs[i]], axis=0).astype(BF16), ks[i].astype(BF16)) for i in rng]
    lmats = [jnp.where(stricts[i], kk[i][:c] * decay[i], 0.0) for i in rng]
    amats = [(kk[i][c:] * decay[i]).astype(BF16) for i in rng]
    tinvs = _unit_triangular_inverse(lmats, ij_xor)
    rhs = [jnp.concatenate([vs[i] * betas[i], kbs[i] * egs[i]], axis=1).astype(BF16) for i in rng]
    uw = [dot(tinvs[i].astype(BF16), rhs[i]).astype(BF16) for i in rng]
    auw = [dot(amats[i], uw[i]) for i in rng]
    kuw = [lax.dot_general((ks[i] * kes[i]).astype(BF16), uw[i], (((0,), (0,)), ((), ())),
                           preferred_element_type=F32) for i in rng]
    q2 = [(qs[i] * egs[i] - auw[i][:, HEAD:]).astype(BF16) for i in rng]
    states = [state_ref[i] for i in rng]
    s16 = [st.astype(BF16) for st in states]
    out = [auw[i][:, :HEAD] + dot(q2[i], s16[i]) for i in rng]
    new_states = [states[i] * cds[i] + kuw[i][:, :HEAD] - dot(kuw[i][:, HEAD:].astype(BF16), s16[i]) for i in rng]
    for i in rng:
        o_ref, sl = outs[i]
        o_ref[:, sl] = out[i]
        state_ref[i] = new_states[i]


def _gdn_scan(qkv, graw, a_row, dt_row, *, n_batch, seq, ctx_len, gw):
    m = qkv.shape[0]
    c = GDN_CHUNK
    n_heads = gw // HEAD
    n_ctx = ctx_len // c
    n_lat = seq // c

    def row_block(b, t, d):
        tl = t - n_ctx
        c_ctx = t if d == 0 else n_ctx - 1 - t
        c_lat = tl if d == 0 else n_lat - 1 - tl
        return jnp.where(t < n_ctx, (n_batch * seq + b * ctx_len) // c + c_ctx, (b * seq) // c + c_lat)

    def spec(col, d):
        return pl.BlockSpec((c, gw), lambda b, t: (row_block(b, t, d), col))

    def gspec(d):
        return pl.BlockSpec((c, LANES), lambda b, t: (row_block(b, t, d), 0))

    const = pl.BlockSpec((1, LANES), lambda b, t: (0, 0))
    in_specs = []
    for d in range(2):
        in_specs += [spec(0, d), spec(1, d), spec(2, d), gspec(d)]
    in_specs += [const, const]
    args = [qkv, qkv, qkv, graw, qkv, qkv, qkv, graw, a_row, dt_row]
    return pl.pallas_call(
        functools.partial(_gdn_scan_kernel, n_heads=n_heads),
        out_shape=[jax.ShapeDtypeStruct((m, gw), F32), jax.ShapeDtypeStruct((m, gw), F32)],
        grid=(n_batch, n_ctx + n_lat),
        in_specs=in_specs,
        out_specs=[spec(0, 0), spec(0, 1)],
        scratch_shapes=[pltpu.VMEM((2 * n_heads, HEAD, HEAD), F32)],
        compiler_params=_params(("parallel", "arbitrary")),
        name="gdn_scan",
    )(*args)


def _gdn_out_kernel(of_ref, ob_ref, z_ref, w_ref, o_ref, *, n_heads):
    w = w_ref[...]
    for h in range(n_heads):
        sl = slice(h * HEAD, (h + 1) * HEAD)
        o = _rms(of_ref[:, sl] + ob_ref[:, sl], w)
        z = z_ref[:, sl]
        o_ref[:, sl] = (o * (z / (1.0 + jnp.exp(-z)))).astype(o_ref.dtype)


def _gdn_out(o_fwd, o_bwd, u, w, *, gw):
    m = o_fwd.shape[0]
    tr = ROW_BLOCK
    blk = lambda col: pl.BlockSpec((tr, gw), lambda i: (i, col))
    return pl.pallas_call(
        functools.partial(_gdn_out_kernel, n_heads=gw // HEAD),
        out_shape=jax.ShapeDtypeStruct((m, gw), BF16),
        grid=(m // tr,),
        in_specs=[blk(0), blk(0), blk(5), pl.BlockSpec((1, HEAD), lambda i: (0, 0))],
        out_specs=blk(0),
        compiler_params=_params(("parallel",)),
        name="gdn_out",
    )(o_fwd, o_bwd, u, w.reshape(1, HEAD))


def _diff_prep_kernel(x_ref, w_ref, cos_ref, sin_ref, o_ref, *, n_heads):
    j = pl.program_id(1)

    @pl.when(j < 2)
    def _():
        w = w_ref[0]
        cos = cos_ref[...]
        sin = sin_ref[...]
        lane = lax.broadcasted_iota(jnp.int32, (x_ref.shape[0], HEAD), 1)
        low = lane < DIFF_DIM
        first_half = (lane % DIFF_DIM) < DIFF_DIM // 2
        scale = jnp.where(j == 0, DIFF_DIM ** -0.5 * math.log2(math.e), 1.0)
        for h in range(n_heads):
            sl = slice(h * HEAD, (h + 1) * HEAD)
            x = x_ref[:, sl]
            sq = x * x
            s_low = jnp.sum(jnp.where(low, sq, 0.0), axis=-1, keepdims=True)
            s_high = jnp.sum(jnp.where(low, 0.0, sq), axis=-1, keepdims=True)
            ms = jnp.where(low, s_low, s_high) * (1.0 / DIFF_DIM)
            y = x * lax.rsqrt(ms + EPS) * w
            swapped = jnp.where(first_half, pltpu.roll(y, HEAD - DIFF_DIM // 2, axis=1),
                                pltpu.roll(y, DIFF_DIM // 2, axis=1))
            o_ref[:, sl] = ((y * cos + swapped * sin) * scale).astype(o_ref.dtype)

    @pl.when(j == 2)
    def _():
        o_ref[...] = x_ref[...].astype(o_ref.dtype)


def _diff_prep(u, w_qk, cos_t, sin_t, *, n_batch, seq, gw):
    m = u.shape[0]
    tr = ROW_BLOCK
    lat_blocks = n_batch * seq // tr
    seq_blocks = seq // tr
    rope_map = lambda i, j: (jnp.where(i < lat_blocks, i % seq_blocks, seq_blocks), 0)
    return pl.pallas_call(
        functools.partial(_diff_prep_kernel, n_heads=gw // HEAD),
        out_shape=jax.ShapeDtypeStruct((m, 3 * gw), BF16),
        grid=(m // tr, 3),
        in_specs=[pl.BlockSpec((tr, gw), lambda i, j: (i, j)),
                  pl.BlockSpec((1, 1, HEAD), lambda i, j: (jnp.minimum(j, 1), 0, 0)),
                  pl.BlockSpec((tr, HEAD), rope_map),
                  pl.BlockSpec((tr, HEAD), rope_map)],
        out_specs=pl.BlockSpec((tr, gw), lambda i, j: (i, j)),
        compiler_params=_params(("parallel", "parallel")),
        name="diff_prep",
    )(u, w_qk, cos_t, sin_t)


def _diff_attn_kernel(*refs, n_seg):
    q_ref = refs[0]
    k_refs = refs[1:1 + n_seg]
    v_refs = refs[1 + n_seg:1 + 2 * n_seg]
    lam_ref, w_ref, o_ref = refs[1 + 2 * n_seg:]
    tq = q_ref.shape[0]
    ts = min(tq, DIFF_SUB_ROWS)
    subs = range(tq // ts)
    lane = lax.broadcasted_iota(jnp.int32, (ts, HEAD), 1)
    zero = jnp.zeros((ts, HEAD), q_ref.dtype)
    v1s = [jnp.concatenate([v_ref[...], jnp.ones(v_ref.shape, v_ref.dtype)], axis=1) for v_ref in v_refs]
    q2s = []
    for i in subs:
        q = q_ref[i * ts:(i + 1) * ts, :]
        q2s.append(jnp.concatenate([jnp.where(lane < DIFF_DIM, q, zero), jnp.where(lane < DIFF_DIM, zero, q)], axis=0))
    scores = [[_dot_t(q2, k_ref[...]) for k_ref in k_refs] for q2 in q2s]
    mxs = [functools.reduce(jnp.maximum, [jnp.max(s, axis=-1, keepdims=True) for s in sc]) for sc in scores]
    es = [[jnp.exp2((s - mx).astype(BF16)) for s in sc] for sc, mx in zip(scores, mxs)]
    accs = [functools.reduce(jnp.add, [jnp.dot(e, v1, preferred_element_type=F32) for e, v1 in zip(e_seg, v1s)])
            for e_seg in es]
    for i, acc in zip(subs, accs):
        r = acc[:, :HEAD] / acc[:, HEAD:HEAD + 1]
        o = r[:ts] - lam_ref[0:1, 0:1] * r[ts:]
        o_ref[i * ts:(i + 1) * ts, :] = _rms(o, w_ref[...]).astype(o_ref.dtype)


def _diff_attention(dp, lam_row, w_sub, *, n_batch, seq, ctx_len, n_heads, latent):
    kcol = n_heads
    vcol = 2 * n_heads
    ctx_blk = n_batch * seq // ctx_len
    if latent:
        tq = min(seq, DIFF_Q_ROWS)
        nq = seq // tq
        qmap = lambda b, h, i: (b * nq + i, h)
        segs = [(seq, lambda col: (lambda b, h, i: (b, col + h))),
                (ctx_len, lambda col: (lambda b, h, i: (ctx_blk + b, col + h)))]
        out_rows = n_batch * seq
        omap = qmap
    else:
        tq = ctx_len
        nq = 1
        qmap = lambda b, h, i: (ctx_blk + b, h)
        segs = [(ctx_len, lambda col: (lambda b, h, i: (ctx_blk + b, col + h)))]
        out_rows = n_batch * ctx_len
        omap = lambda b, h, i: (b, h)
    in_specs = [pl.BlockSpec((tq, HEAD), qmap)]
    in_specs += [pl.BlockSpec((rows, HEAD), mk(kcol)) for rows, mk in segs]
    in_specs += [pl.BlockSpec((rows, HEAD), mk(vcol)) for rows, mk in segs]
    in_specs += [pl.BlockSpec((1, HEAD), lambda b, h, i: (0, 0))] * 2
    n_seg = len(segs)
    return pl.pallas_call(
        functools.partial(_diff_attn_kernel, n_seg=n_seg),
        out_shape=jax.ShapeDtypeStruct((out_rows, n_heads * HEAD), BF16),
        grid=(n_batch, n_heads, nq),
        in_specs=in_specs,
        out_specs=pl.BlockSpec((tq, HEAD), omap),
        compiler_params=_params(("parallel", "parallel", "arbitrary")),
        name="diff_attn_latent" if latent else "diff_attn_context",
    )(*([dp] * (1 + 2 * n_seg)), lam_row, w_sub)


def _rope_tables(seq, extra_rows):
    n_freq = DIFF_DIM // 4
    inv = ROPE_BASE ** (-jnp.arange(n_freq, dtype=F32) / n_freq)
    t = jnp.arange(seq)
    row = (t // GRID_W).astype(F32)
    col = (t % GRID_W).astype(F32)
    ang = jnp.concatenate([row[:, None] * inv, col[:, None] * inv], axis=-1)
    cos, sin = jnp.cos(ang), jnp.sin(ang)
    cos_t = jnp.tile(cos, (1, HEAD // cos.shape[1]))
    sin_t = jnp.tile(jnp.concatenate([-sin, sin], axis=-1), (1, HEAD // (2 * sin.shape[1])))
    cos_t = jnp.concatenate([cos_t, jnp.ones((extra_rows, HEAD), F32)], axis=0)
    sin_t = jnp.concatenate([sin_t, jnp.zeros((extra_rows, HEAD), F32)], axis=0)
    return cos_t, sin_t


def _kernel_impl(x, c, ctx, c_ctx, ada_down, ada_up, ada_bias, norm1, w_in, na_q_norm, na_k_norm, na_rpb,
                 gdn_conv, gdn_a_log, gdn_dt_bias, gdn_norm, diff_q_norm, diff_k_norm, diff_lambda, diff_subln,
                 w_out, norm2, w_ff1, w_ff2):
    n_batch, seq, d = x.shape
    ctx_len = ctx.shape[1]
    depth = w_in.shape[0]
    d_ff = w_ff1.shape[2]
    naw = d // 4
    gw = 3 * d // 8
    na_heads = naw // HEAD
    g_heads = gw // HEAD
    n_gate = 4 * g_heads
    m_lat = n_batch * seq
    m_all = m_lat + n_batch * ctx_len
    n_main = 9 * gw
    gate_off = 3 * naw + 4 * gw
    assert w_in.shape[2] == n_main + n_gate and n_gate <= LANES
    assert seq % (NA_ROWS * GRID_W) == 0 and ctx_len % ROW_BLOCK == 0 and seq % ctx_len == 0

    xs = jnp.concatenate([x.reshape(m_lat, d), ctx.reshape(n_batch * ctx_len, d)], axis=0)
    cond = jnp.concatenate([c, c_ctx[None, :]], axis=0)
    cond = cond * (1.0 / (1.0 + jnp.exp(-cond)))
    cond = jnp.pad(cond, ((0, 16 - (n_batch + 1)), (0, 0))).astype(BF16)
    cos_t, sin_t = _rope_tables(seq, ROW_BLOCK)
    dims = dict(n_batch=n_batch, seq=seq, ctx_len=ctx_len)
    tm = 512
    w_in16 = w_in.astype(BF16)
    w_diff16 = w_in16[:, :, gate_off + n_gate:]
    w_gate16 = jnp.pad(w_in16[:, :, gate_off:gate_off + n_gate], ((0, 0), (0, 0), (0, LANES - n_gate)))
    w_out16 = w_out.astype(BF16)
    w_ff2_16 = w_ff2.astype(BF16)

    for l in range(depth):
        last = l == depth - 1
        lam_init = 0.8 - 0.6 * math.exp(-0.3 * l)
        a_row = jnp.pad(-jnp.exp(gdn_a_log[l]).reshape(1, -1), ((0, 0), (2 * g_heads, LANES - n_gate)))
        dt_row = jnp.pad(gdn_dt_bias[l].reshape(1, -1), ((0, 0), (2 * g_heads, LANES - n_gate)))
        lv = diff_lambda[l]
        lam = jnp.exp(jnp.sum(lv[0] * lv[1])) - jnp.exp(jnp.sum(lv[2] * lv[3])) + lam_init
        lam_row = jnp.full((1, HEAD), lam, F32)
        w_sub = (diff_subln[l] * (1.0 - lam_init)).reshape(1, HEAD)
        w_qk = jnp.stack([jnp.tile(diff_q_norm[l], 2), jnp.tile(diff_k_norm[l], 2)]).reshape(2, 1, HEAD)
        bias = _na_bias_tables(na_rpb[l])

        low = _matmul(cond, ada_down[l].astype(BF16), tm=16, tn=ada_down.shape[2], tk=d, out_dtype=BF16,
                      name="ada_down")
        mod = _matmul(low, ada_up[l].astype(BF16), tm=16, tn=_pick_tile(N_MOD * d, 4096), tk=ada_down.shape[2],
                      out_dtype=F32, name="ada_up")
        mod = (mod[:n_batch + 1] + ada_bias[l][None, :]).reshape(n_batch + 1, N_MOD, d)

        h1 = _modulate(xs, norm1[l], mod, 0, seq=seq, n_batch=n_batch)
        u = _matmul(h1, w_in16, layer=l, n_cols=gate_off, tm=tm, tn=_pick_tile(gate_off, 1536), tk=d,
                    out_dtype=F32, name="w_in")
        u_diff = _matmul(h1, w_diff16, layer=l, n_cols=3 * gw, tm=tm, tn=_pick_tile(3 * gw, 1536), tk=d,
                         out_dtype=F32, name="w_in_diff")
        graw = _matmul(h1, w_gate16, layer=l, n_cols=LANES, tm=tm, tn=LANES, tk=d, out_dtype=F32, name="w_in_gates")

        na_lat = _na_latent(u, bias, na_q_norm[l], na_k_norm[l], n_heads=na_heads, **dims)
        qkv = _gdn_conv(u, gdn_conv[l], gw=gw, **dims)
        o_fwd, o_bwd = _gdn_scan(qkv, graw, a_row, dt_row, gw=gw, **dims)
        gdn = _gdn_out(o_fwd, o_bwd, u, gdn_norm[l], gw=gw)
        dp = _diff_prep(u_diff, w_qk, cos_t, sin_t, n_batch=n_batch, seq=seq, gw=gw)
        df_lat = _diff_attention(dp, lam_row, w_sub, n_heads=g_heads, latent=True, **dims)

        if last:
            mix = jnp.concatenate([na_lat, gdn[:m_lat], df_lat], axis=1)
        else:
            na_ctx = _na_context(u, na_q_norm[l], na_k_norm[l], n_heads=na_heads, **dims)
            df_ctx = _diff_attention(dp, lam_row, w_sub, n_heads=g_heads, latent=False, **dims)
            mix = jnp.concatenate([jnp.concatenate([na_lat, na_ctx], axis=0), gdn,
                                   jnp.concatenate([df_lat, df_ctx], axis=0)], axis=1)

        xs = _matmul(mix, w_out16, layer=l, n_cols=d, tm=tm, tn=_pick_tile(d, 1024), tk=d, out_dtype=F32,
                     epilogue="gated_res", res=xs, mod=mod, gate_row=2, seq=seq, n_batch=n_batch, name="w_out")
        h2 = _modulate(xs, norm2[l], mod, 3, seq=seq, n_batch=n_batch)
        ff = _matmul(h2, w_ff1, layer=l, n_cols=d_ff, cast_w=True, tm=tm, tn=_pick_tile(d_ff, W_CAST_TILE), tk=d,
                     out_dtype=BF16, epilogue="relu2", name="w_ff1")
        xs = _matmul(ff, w_ff2_16, layer=l, n_cols=d, tm=tm, tn=2 * LANES, tk=d_ff, rows_outer=True, out_dtype=F32,
                     epilogue="gated_res", res=xs, mod=mod, gate_row=5, seq=seq, n_batch=n_batch, name="w_ff2")

    return xs[:m_lat].reshape(n_batch, seq, d)


@jax.jit
def kernel(x, c, ctx, c_ctx, ada_down, ada_up, ada_bias, norm1, w_in, na_q_norm, na_k_norm, na_rpb,
           gdn_conv, gdn_a_log, gdn_dt_bias, gdn_norm, diff_q_norm, diff_k_norm, diff_lambda, diff_subln,
           w_out, norm2, w_ff1, w_ff2):
    return _kernel_impl(x, c, ctx, c_ctx, ada_down, ada_up, ada_bias, norm1, w_in, na_q_norm, na_k_norm, na_rpb,
                        gdn_conv, gdn_a_log, gdn_dt_bias, gdn_norm, diff_q_norm, diff_k_norm, diff_lambda,
                        diff_subln, w_out, norm2, w_ff1, w_ff2)
```
